```python
import jax, jax.numpy as jnp
from jax import lax
import numpy as np

D_MODEL = 2048
BATCH = 16
SEQ = 2048
DEPTH = 2
DEC_BATCH = 2
DEC_SEQ = 16384
PAST_LEN = 128

MIX_W = D_MODEL
F_W = MIX_W // 4
F_GROUPS = 4
F_GW = F_W // F_GROUPS
M_W = MIX_W - F_W
M_HEADS = 8
M_DV = M_W // M_HEADS
M_DK = M_DV // 2
QK_W = M_HEADS * M_DK
N_IN = F_W + 2 * QK_W + 2 * M_W + 4 * M_HEADS
CONV_W = 3
CHUNK = 64
N_GROUPS = 4
EXP_PER_GROUP = 8
N_EXPERTS = N_GROUPS * EXP_PER_GROUP
TOP_K = 2
D_EXPERT = D_MODEL // 2
MOE_BLOCK = 256
ALPHA = (2 * DEPTH) ** 0.25
BETA = (8 * DEPTH) ** -0.25
LN_EPS = 1e-5
HN_EPS = 1e-6

kernel_name = "hybrid_fnet_mlstm_hmoe_encoder"


def _layer_norm(x, g, b):
    xf = x.astype(jnp.float32)
    mu = jnp.mean(xf, -1, keepdims=True)
    var = jnp.mean(jnp.square(xf - mu), -1, keepdims=True)
    y = (xf - mu) * lax.rsqrt(var + LN_EPS) * g.astype(jnp.float32) + b.astype(jnp.float32)
    return y.astype(x.dtype)


def _centred_conv(x, w):
    xp = jnp.pad(x, ((0, 0), (1, 1), (0, 0)))
    return xp[:, :-2] * w[0] + xp[:, 1:-1] * w[1] + xp[:, 2:] * w[2]


def _to_chunks(a):
    b, s = a.shape[:2]
    a = a.reshape((b, s // CHUNK, CHUNK) + a.shape[2:])
    perm = (1, 0, 3, 2) + tuple(range(4, a.ndim))
    return jnp.transpose(a, perm)


def _mlstm_chunkwise(q, k, v, ig, lf):
    bsz, s, nh, dk = q.shape
    dv = v.shape[-1]
    xs = (_to_chunks(q), _to_chunks(k), _to_chunks(v), _to_chunks(ig), _to_chunks(lf))
    lower = jnp.tril(jnp.ones((CHUNK, CHUNK), dtype=bool))

    def step(carry, inp):
        c_st, n_st, m_st = carry
        qc, kc, vc, ic, fc = inp
        b = jnp.cumsum(fc, axis=-1)
        dmat = jnp.where(lower, b[..., :, None] - b[..., None, :] + ic[..., None, :], -jnp.inf)
        inter = b + m_st[..., None]
        m_row = jnp.maximum(inter, jnp.max(dmat, -1))
        s_mat = jnp.einsum('bhld,bhsd->bhls', qc, kc) * jnp.exp(dmat - m_row[..., None])
        w_inter = jnp.exp(inter - m_row)
        num = jnp.einsum('bhls,bhsv->bhlv', s_mat, vc) + w_inter[..., None] * jnp.einsum('bhld,bhdv->bhlv', qc, c_st)
        den = jnp.sum(s_mat, -1) + w_inter * jnp.einsum('bhld,bhd->bhl', qc, n_st)
        h = num / jnp.maximum(jnp.abs(den), jnp.exp(-m_row))[..., None]
        g_tot = b[..., -1]
        w_log = g_tot[..., None] - b + ic
        m_new = jnp.maximum(g_tot + m_st, jnp.max(w_log, -1))
        decay = jnp.exp(g_tot + m_st - m_new)
        kw = kc * jnp.exp(w_log - m_new[..., None])[..., None]
        c_new = decay[..., None, None] * c_st + jnp.einsum('bhsd,bhsv->bhdv', kw, vc)
        n_new = decay[..., None] * n_st + jnp.sum(kw, axis=2)
        return (c_new, n_new, m_new), h

    init = (jnp.zeros((bsz, nh, dk, dv), jnp.float32),
            jnp.zeros((bsz, nh, dk), jnp.float32),
            jnp.zeros((bsz, nh), jnp.float32))
    _, hs = lax.scan(step, init, xs)
    return jnp.transpose(hs, (1, 0, 3, 2, 4)).reshape(bsz, s, nh, dv)


def _mixer(h, w_in, conv_qk, b_igate, b_fgate, hn_g, w_out):
    bsz, s, _ = h.shape
    f32 = jnp.float32
    z = h @ w_in
    o1 = F_W
    o3 = o1 + 2 * QK_W
    o4 = o3 + M_W
    o5 = o4 + M_W
    zf = z[..., :o1].astype(f32).reshape(bsz, s, F_GROUPS, F_GW)
    four = jnp.real(jnp.fft.fftn(zf, axes=(1, 3), norm='ortho')).reshape(bsz, s, F_W)
    qk = jax.nn.silu(_centred_conv(z[..., o1:o3], conv_qk)).astype(f32)
    q = qk[..., :QK_W].reshape(bsz, s, M_HEADS, M_DK) * (M_DK ** -0.5)
    k = qk[..., QK_W:].reshape(bsz, s, M_HEADS, M_DK)
    v = z[..., o3:o4].astype(f32).reshape(bsz, s, M_HEADS, M_DV)
    o_gate = jax.nn.sigmoid(z[..., o4:o5].astype(f32))
    gates = z[..., o5:].astype(f32).reshape(bsz, s, 2, 2, M_HEADS)
    ig = gates[:, :, 0] + b_igate.astype(f32)
    lf = jax.nn.log_sigmoid(gates[:, :, 1] + b_fgate.astype(f32))
    h_fwd = _mlstm_chunkwise(q, k, v, ig[:, :, 0], lf[:, :, 0])
    rev = lambda a: jnp.flip(a, axis=1)
    h_bwd = rev(_mlstm_chunkwise(rev(q), rev(k), rev(v), rev(ig[:, :, 1]), rev(lf[:, :, 1])))
    hsum = h_fwd + h_bwd
    mu = jnp.mean(hsum, -1, keepdims=True)
    var = jnp.mean(jnp.square(hsum - mu), -1, keepdims=True)
    hn = ((hsum - mu) * lax.rsqrt(var + HN_EPS)).reshape(bsz, s, M_W) * hn_g.astype(f32)
    m_out = o_gate * hn
    cat = jnp.concatenate([four, m_out], -1).astype(h.dtype)
    return cat @ w_out


def _hier_moe(x2d, w_coarse, b_coarse, w_fine, b_fine, w_e_gate, w_e_up, w_e_down):
    t, d = x2d.shape
    f32 = jnp.float32
    xf = x2d.astype(f32)
    lc = xf @ w_coarse.astype(f32) + b_coarse.astype(f32)
    grp = jnp.argmax(lc, -1)
    p_grp = jnp.take_along_axis(jax.nn.softmax(lc, -1), grp[:, None], 1)[:, 0]
    lfine = (xf @ w_fine.astype(f32) + b_fine.astype(f32)).reshape(t, N_GROUPS, EXP_PER_GROUP)
    lsel = jnp.take_along_axis(lfine, grp[:, None, None], 1)[:, 0]
    top_v, top_j = lax.top_k(lsel, TOP_K)
    gate = (p_grp[:, None] * jax.nn.softmax(top_v, -1)).reshape(-1)
    eid = (grp[:, None] * EXP_PER_GROUP + top_j).reshape(-1).astype(jnp.int32)
    tok = jnp.repeat(jnp.arange(t, dtype=jnp.int32), TOP_K)
    n_assign = t * TOP_K
    order = jnp.argsort(eid)
    e_s, t_s, g_s = eid[order], tok[order], gate[order]
    counts = jnp.bincount(eid, length=N_EXPERTS)
    start = jnp.cumsum(counts) - counts
    padded = (counts + MOE_BLOCK - 1) // MOE_BLOCK * MOE_BLOCK
    pend = jnp.cumsum(padded)
    pstart = pend - padded
    dest = pstart[e_s] + jnp.arange(n_assign, dtype=jnp.int32) - start[e_s]
    n_blocks = -(-n_assign // MOE_BLOCK) + N_EXPERTS
    buf_tok = jnp.full((n_blocks * MOE_BLOCK,), t, jnp.int32).at[dest].set(t_s)
    buf_gate = jnp.zeros((n_blocks * MOE_BLOCK,), f32).at[dest].set(g_s)
    blk_e = jnp.minimum(jnp.searchsorted(pend, jnp.arange(n_blocks, dtype=jnp.int32) * MOE_BLOCK, side='right'), N_EXPERTS - 1)
    x_pad = jnp.concatenate([x2d, jnp.zeros((1, d), x2d.dtype)], 0)

    def expert_block(args):
        bt, bg, e = args
        xb = x_pad[bt]
        hid = jax.nn.silu(xb @ w_e_gate[e]) * (xb @ w_e_up[e])
        return ((hid @ w_e_down[e]).astype(f32) * bg[:, None]).astype(x2d.dtype)

    yb = lax.map(expert_block, (buf_tok.reshape(n_blocks, MOE_BLOCK), buf_gate.reshape(n_blocks, MOE_BLOCK), blk_e))
    out = jnp.zeros((t + 1, d), x2d.dtype).at[buf_tok].add(yb.reshape(-1, d))
    return out[:t]


def _encode(x, ln_in_g, ln_in_b, w_in, conv_qk, b_igate, b_fgate, hn_g, w_out, ln1_g, ln1_b,
            w_coarse, b_coarse, w_fine, b_fine, w_e_gate, w_e_up, w_e_down, ln2_g, ln2_b):
    bsz, s, d = x.shape
    x = _layer_norm(x, ln_in_g, ln_in_b)
    for l in range(DEPTH):
        mix = _mixer(x, w_in[l], conv_qk[l], b_igate[l], b_fgate[l], hn_g[l], w_out[l])
        x = _layer_norm(ALPHA * x + mix, ln1_g[l], ln1_b[l])
        ffn = _hier_moe(x.reshape(bsz * s, d), w_coarse[l], b_coarse[l], w_fine[l], b_fine[l],
                        w_e_gate[l], w_e_up[l], w_e_down[l]).reshape(bsz, s, d)
        x = _layer_norm(ALPHA * x + ffn, ln2_g[l], ln2_b[l])
    return x


def setup_inputs(seed: int = 0) -> dict:
    key = jax.random.key(seed)
    ks = jax.random.split(key, 24)
    f32 = jnp.float32
    nrm = lambda k, shape, scale: jax.random.normal(k, shape, f32) * scale
    col = jnp.arange(N_IN)
    v_lo = F_W + 2 * QK_W
    value_cols = (col < F_W) | ((col >= v_lo) & (col < v_lo + M_W))
    col_scale = jnp.where(value_cols, BETA, 1.0).astype(f32)
    return {
        'x_prompt': nrm(ks[0], (BATCH, SEQ, D_MODEL), 1.0),
        'x_sample': nrm(ks[1], (DEC_BATCH, DEC_SEQ, D_MODEL), 1.0),
        'ln_in_g': 1.0 + nrm(ks[2], (D_MODEL,), 0.02),
        'ln_in_b': nrm(ks[3], (D_MODEL,), 0.02),
        'w_in': nrm(ks[4], (DEPTH, D_MODEL, N_IN), D_MODEL ** -0.5) * col_scale,
        'conv_qk': nrm(ks[5], (DEPTH, CONV_W, 2 * QK_W), CONV_W ** -0.5),
        'b_igate': nrm(ks[6], (DEPTH, 2, M_HEADS), 0.1),
        'b_fgate': 3.0 + nrm(ks[7], (DEPTH, 2, M_HEADS), 0.5),
        'hn_g': 1.0 + nrm(ks[8], (DEPTH, M_W), 0.02),
        'w_out': nrm(ks[9], (DEPTH, MIX_W, D_MODEL), MIX_W ** -0.5) * BETA,
        'ln1_g': 1.0 + nrm(ks[10], (DEPTH, D_MODEL), 0.02),
        'ln1_b': nrm(ks[11], (DEPTH, D_MODEL), 0.02),
        'w_coarse': nrm(ks[12], (DEPTH, D_MODEL, N_GROUPS), D_MODEL ** -0.5),
        'b_coarse': nrm(ks[13], (DEPTH, N_GROUPS), 0.01),
        'w_fine': nrm(ks[14], (DEPTH, D_MODEL, N_EXPERTS), D_MODEL ** -0.5),
        'b_fine': nrm(ks[15], (DEPTH, N_EXPERTS), 0.01),
        'w_e_gate': nrm(ks[16], (DEPTH, N_EXPERTS, D_MODEL, D_EXPERT), D_MODEL ** -0.5),
        'w_e_up': nrm(ks[17], (DEPTH, N_EXPERTS, D_MODEL, D_EXPERT), D_MODEL ** -0.5),
        'w_e_down': nrm(ks[18], (DEPTH, N_EXPERTS, D_EXPERT, D_MODEL), D_EXPERT ** -0.5) * BETA,
        'ln2_g': 1.0 + nrm(ks[19], (DEPTH, D_MODEL), 0.02),
        'ln2_b': nrm(ks[20], (DEPTH, D_MODEL), 0.02),
    }


def reference(x_prompt, x_sample, ln_in_g, ln_in_b, w_in, conv_qk, b_igate, b_fgate, hn_g, w_out,
              ln1_g, ln1_b, w_coarse, b_coarse, w_fine, b_fine, w_e_gate, w_e_up, w_e_down, ln2_g, ln2_b):
    y_prompt = _encode(x_prompt, ln_in_g, ln_in_b, w_in, conv_qk, b_igate, b_fgate, hn_g, w_out, ln1_g, ln1_b,
                       w_coarse, b_coarse, w_fine, b_fine, w_e_gate, w_e_up, w_e_down, ln2_g, ln2_b)
    y_sample = _encode(x_sample, ln_in_g, ln_in_b, w_in, conv_qk, b_igate, b_fgate, hn_g, w_out, ln1_g, ln1_b,
                       w_coarse, b_coarse, w_fine, b_fine, w_e_gate, w_e_up, w_e_down, ln2_g, ln2_b)
    return (y_prompt, y_sample)
```

```python
import functools

import numpy as np
import jax
import jax.numpy as jnp
from jax import lax
from jax.experimental import pallas as pl
from jax.experimental.pallas import tpu as pltpu

F32 = jnp.float32
BF16 = jnp.bfloat16
I32 = jnp.int32

D_MODEL = 2048
F_W = D_MODEL // 4
F_GROUPS = 4
F_GW = F_W // F_GROUPS
M_W = D_MODEL - F_W
M_HEADS = 8
M_DV = M_W // M_HEADS
M_DK = M_DV // 2
QK_W = M_HEADS * M_DK
N_GROUPS = 4
EXP_PER_GROUP = 8
N_EXPERTS = N_GROUPS * EXP_PER_GROUP
DEPTH = 2
ALPHA = (2 * DEPTH) ** 0.25
LN_EPS = 1e-5
HN_EPS = 1e-6

LANES = 128
SUBLANES = 8
MIB = 1024 * 1024

TM_IN = 256
TM_OUT = 512
CHUNK = 256
DENSE_DFT_MAX = 4096
TK_DFT = 512
FS2 = 128
FT = 8
MOE_BLK = 512
TD = 256
ZERO_ROWS = 256
GATE_COLS = LANES
ROUTE_COLS = LANES
FINE_OFF = SUBLANES


def _dot(a, b):
    return jnp.dot(a, b, preferred_element_type=F32)


def _dot_nt(a, b):
    return lax.dot_general(a, b, (((1,), (1,)), ((), ())), preferred_element_type=F32)


def _dot_tn(a, b):
    return lax.dot_general(a, b, (((0,), (0,)), ((), ())), preferred_element_type=F32)


def _cparams(sem, vmem_mib):
    return pltpu.CompilerParams(dimension_semantics=sem, vmem_limit_bytes=vmem_mib * MIB)


def _resident(shape):
    nd = len(shape)
    return pl.BlockSpec(shape, lambda *_: (0,) * nd, pipeline_mode=pl.Buffered(1))


def _layer_norm_rows(x, g, b):
    mu = jnp.mean(x, -1, keepdims=True)
    xc = x - mu
    var = jnp.mean(xc * xc, -1, keepdims=True)
    return xc * lax.rsqrt(var + LN_EPS) * g + b


def _log_sigmoid(x):
    return jnp.minimum(x, 0.0) - jnp.log(1.0 + jnp.exp(-jnp.abs(x)))


def _inproj_kernel(apply_ln, x_ref, g_ref, b_ref, wf_ref, wc_ref, wqk_ref, wv_ref, wo_ref, wg_ref, *outs):
    if apply_ln:
        xn_ref, xr_ref, xi_ref, zqk_ref, zv_ref, zo_ref, gcol_ref, grow_ref = outs
    else:
        xr_ref, xi_ref, zqk_ref, zv_ref, zo_ref, gcol_ref, grow_ref = outs
    x = x_ref[...]
    if apply_ln:
        x = _layer_norm_rows(x, g_ref[...], b_ref[...])
        xn_ref[...] = x
    xb = x.astype(BF16)
    zf = _dot(xb, wf_ref[...]).astype(BF16)
    wc = wc_ref[...]
    for g in range(F_GROUPS):
        c = _dot(zf[:, g * F_GW:(g + 1) * F_GW], wc)
        xr_ref[:, g * F_GW:(g + 1) * F_GW] = c[:, :F_GW]
        xi_ref[:, g * F_GW:(g + 1) * F_GW] = c[:, F_GW:]
    zqk_ref[...] = _dot(xb, wqk_ref[...])
    zv_ref[...] = _dot(xb, wv_ref[...]).astype(BF16)
    zo_ref[...] = _dot(xb, wo_ref[...])
    zg = _dot(xb, wg_ref[...])
    gcol_ref[...] = zg
    grow_ref[...] = zg.T


def _inproj(x2d, ln_g, ln_b, wf, wc, wqk, wv, wo, wg, apply_ln):
    t = x2d.shape[0]
    tm = TM_IN
    row = lambda w: pl.BlockSpec((tm, w), lambda i: (i, 0))
    out_shape = [
        jax.ShapeDtypeStruct((t, F_W), F32), jax.ShapeDtypeStruct((t, F_W), F32),
        jax.ShapeDtypeStruct((t, 2 * QK_W), F32), jax.ShapeDtypeStruct((t, M_W), BF16),
        jax.ShapeDtypeStruct((t, M_W), F32), jax.ShapeDtypeStruct((t, GATE_COLS), F32),
        jax.ShapeDtypeStruct((GATE_COLS, t), F32),
    ]
    out_specs = [row(F_W), row(F_W), row(2 * QK_W), row(M_W), row(M_W), row(GATE_COLS),
                 pl.BlockSpec((GATE_COLS, tm), lambda i: (0, i))]
    if apply_ln:
        out_shape = [jax.ShapeDtypeStruct((t, D_MODEL), F32)] + out_shape
        out_specs = [row(D_MODEL)] + out_specs
    return pl.pallas_call(
        functools.partial(_inproj_kernel, apply_ln),
        grid=(t // tm,),
        in_specs=[row(D_MODEL), _resident((1, D_MODEL)), _resident((1, D_MODEL)),
                  _resident(wf.shape), _resident(wc.shape), _resident(wqk.shape),
                  _resident(wv.shape), _resident(wo.shape), _resident(wg.shape)],
        out_specs=out_specs, out_shape=out_shape,
        compiler_params=_cparams(("parallel",), 56),
        name="inproj",
    )(x2d, ln_g, ln_b, wf, wc, wqk, wv, wo, wg)


def _dft_dense_kernel(fc_ref, fs_ref, xr_ref, xi_ref, o_ref):
    o_ref[0] = (_dot(fc_ref[...], xr_ref[0].astype(BF16))
                + _dot(fs_ref[...], xi_ref[0].astype(BF16)))


def _dft_dense(xr, xi):
    b, s, w = xr.shape
    tk = min(TK_DFT, s)
    k = jnp.arange(s, dtype=I32)
    ang = (2.0 * np.pi / s) * ((k[:, None] * k[None, :]) % s).astype(F32)
    scale = 1.0 / np.sqrt(s)
    fc = (jnp.cos(ang) * scale).astype(BF16)
    fs = (jnp.sin(ang) * scale).astype(BF16)
    return pl.pallas_call(
        _dft_dense_kernel,
        grid=(b, s // tk),
        in_specs=[pl.BlockSpec((tk, s), lambda i, j: (j, 0)), pl.BlockSpec((tk, s), lambda i, j: (j, 0)),
                  pl.BlockSpec((1, s, w), lambda i, j: (i, 0, 0)), pl.BlockSpec((1, s, w), lambda i, j: (i, 0, 0))],
        out_specs=pl.BlockSpec((1, tk, w), lambda i, j: (i, j, 0)),
        out_shape=jax.ShapeDtypeStruct((b, s, w), F32),
        compiler_params=_cparams(("parallel", "arbitrary"), 48),
        name="dft_dense",
    )(fc, fs, xr, xi)


def _dft_stage1_kernel(f_ref, xr_ref, xi_ref, ur_ref, ui_ref):
    s1 = xr_ref.shape[1]
    f = f_ref[...]
    for j in range(FT):
        d = jnp.concatenate([xr_ref[0, :, j, :], xi_ref[0, :, j, :]], axis=0).astype(BF16)
        u = _dot(f, d)
        ur_ref[0, :, j, :] = u[:s1]
        ui_ref[0, :, j, :] = u[s1:]


def _dft_stage2_kernel(t_ref, ur_ref, ui_ref, o_ref):
    for j in range(FT):
        d = jnp.concatenate([ur_ref[0, j], ui_ref[0, j]], axis=0).astype(BF16)
        o_ref[0, :, j, :] = _dot(t_ref[j], d)


def _dft_two_stage(xr, xi):
    b, s, w = xr.shape
    s1 = s // FS2
    assert s1 * FS2 == s and s1 % FT == 0 and FS2 % FT == 0
    xr4 = xr.reshape(b, s1, FS2, w)
    xi4 = xi.reshape(b, s1, FS2, w)
    k1 = np.arange(s1, dtype=np.int64)
    a1 = 2.0 * np.pi * ((k1[:, None] * k1[None, :]) % s1) / s1
    c1, sn1 = np.cos(a1), np.sin(a1)
    f1 = jnp.asarray(np.block([[c1, sn1], [-sn1, c1]]) / np.sqrt(s), dtype=BF16)
    s2 = jnp.arange(FS2, dtype=I32)
    kk = jnp.arange(s1, dtype=I32)[:, None] + s1 * jnp.arange(FS2, dtype=I32)[None, :]
    ang = (2.0 * np.pi / s) * ((kk[:, :, None] * s2[None, None, :]) % s).astype(F32)
    t2 = jnp.concatenate([jnp.cos(ang), jnp.sin(ang)], axis=-1).astype(BF16)
    blk1 = pl.BlockSpec((1, s1, FT, w), lambda i, j: (i, 0, j, 0))
    ur, ui = pl.pallas_call(
        _dft_stage1_kernel,
        grid=(b, FS2 // FT),
        in_specs=[pl.BlockSpec((2 * s1, 2 * s1), lambda i, j: (0, 0)), blk1, blk1],
        out_specs=[blk1, blk1],
        out_shape=[jax.ShapeDtypeStruct((b, s1, FS2, w), F32)] * 2,
        compiler_params=_cparams(("parallel", "parallel"), 48),
        name="dft_stage1",
    )(f1, xr4, xi4)
    blk2 = pl.BlockSpec((1, FT, FS2, w), lambda i, j: (i, j, 0, 0))
    y = pl.pallas_call(
        _dft_stage2_kernel,
        grid=(b, s1 // FT),
        in_specs=[pl.BlockSpec((FT, FS2, 2 * FS2), lambda i, j: (j, 0, 0)), blk2, blk2],
        out_specs=pl.BlockSpec((1, FS2, FT, w), lambda i, j: (i, 0, j, 0)),
        out_shape=jax.ShapeDtypeStruct((b, FS2, s1, w), F32),
        compiler_params=_cparams(("parallel", "parallel"), 48),
        name="dft_stage2",
    )(t2, ur, ui)
    return y.reshape(b, s, w)


def _seq_dft(xr, xi):
    if xr.shape[1] <= DENSE_DFT_MAX:
        return _dft_dense(xr, xi)
    return _dft_two_stage(xr, xi)


def _split3(x):
    hi = x.astype(BF16)
    r = x - hi.astype(F32)
    mid = r.astype(BF16)
    lo = (r - mid.astype(F32)).astype(BF16)
    return hi, mid, lo


def _mlstm_kernel(reverse, nc, zqk_ref, hp_ref, hn_ref, cw_ref, zv_ref, gcol_ref, grow_ref,
                  bcol_ref, brow_ref, *rest):
    if reverse:
        hf_ref, zo_ref, hng_ref, out_ref, c_ref, n_ref, m_ref, acc_ref = rest
    else:
        out_ref, c_ref, n_ref, m_ref = rest
    L = zqk_ref.shape[1]
    step = pl.program_id(1)
    cidx = (nc - 1 - step) if reverse else step

    @pl.when(step == 0)
    def _():
        c_ref[...] = jnp.zeros_like(c_ref)
        n_ref[...] = jnp.zeros_like(n_ref)
        m_ref[...] = jnp.zeros_like(m_ref)

    d = 1 if reverse else 0
    i_lo = M_HEADS * d
    f_lo = 2 * M_HEADS + M_HEADS * d

    z = zqk_ref[0]
    rowi = lax.broadcasted_iota(I32, (L, 1), 0)
    prev = jnp.where(cidx > 0, hp_ref[0][SUBLANES - 1:SUBLANES, :], 0.0)
    nxt = jnp.where(cidx < nc - 1, hn_ref[0][0:1, :], 0.0)
    zm1 = jnp.where(rowi == 0, prev, pltpu.roll(z, 1, 0))
    zp1 = jnp.where(rowi == L - 1, nxt, pltpu.roll(z, L - 1, 0))
    cw = cw_ref[...]
    conv = zm1 * cw[0:1] + z * cw[1:2] + zp1 * cw[2:3]
    qk = conv * jax.nn.sigmoid(conv)
    q_all = qk[:, :QK_W] * (M_DK ** -0.5)
    k_all = qk[:, QK_W:]

    gcol = gcol_ref[0] + bcol_ref[...]
    grow = grow_ref[...] + brow_ref[:, 0:1]
    lf_col = _log_sigmoid(gcol)
    lf_row = _log_sigmoid(grow[f_lo:f_lo + M_HEADS, :])
    ig_row = grow[i_lo:i_lo + M_HEADS, :]

    ti = lax.broadcasted_iota(I32, (L, L), 0)
    si = lax.broadcasted_iota(I32, (L, L), 1)
    causal = (si >= ti) if reverse else (si <= ti)
    tri = jnp.where(causal, 1.0, 0.0).astype(BF16)
    tri_t = jnp.where((ti >= si) if reverse else (ti <= si), 1.0, 0.0).astype(BF16)
    a_col = sum(_dot(tri, p) for p in _split3(lf_col))
    a_row = sum(_dot(p, tri_t) for p in _split3(lf_row))
    r_row = ig_row - a_row
    g_tot = a_col[0:1, :] if reverse else a_col[L - 1:L, :]
    ig_al = pltpu.roll(gcol, 2 * M_HEADS, 1)
    m_st = m_ref[...]
    wl_col = g_tot - a_col + ig_al
    m_new = jnp.maximum(g_tot + m_st, jnp.max(wl_col, axis=0, keepdims=True))
    decay = jnp.exp(g_tot + m_st - m_new)
    e_col = jnp.exp(wl_col - m_new)
    inter_col = a_col + m_st

    zv = zv_ref[0]
    for h in range(M_HEADS):
        ch = f_lo + h
        q = q_all[:, h * M_DK:(h + 1) * M_DK]
        k = k_all[:, h * M_DK:(h + 1) * M_DK]
        qb = q.astype(BF16)
        kb = k.astype(BF16)
        vb = zv[:, h * M_DV:(h + 1) * M_DV]
        dm = jnp.where(causal, a_col[:, ch:ch + 1] + r_row[h:h + 1, :], -jnp.inf)
        inter = inter_col[:, ch:ch + 1]
        m_row = jnp.maximum(inter, jnp.max(dm, axis=-1, keepdims=True))
        s_mat = _dot_nt(qb, kb) * jnp.exp(dm - m_row)
        w_inter = jnp.exp(inter - m_row)
        c_st = c_ref[h]
        n_st = n_ref[h:h + 1, :]
        num = _dot(s_mat.astype(BF16), vb) + w_inter * _dot(qb, c_st.astype(BF16))
        den = jnp.sum(s_mat, -1, keepdims=True) + w_inter * jnp.sum(q * n_st, -1, keepdims=True)
        hout = num * (1.0 / jnp.maximum(jnp.abs(den), jnp.exp(-m_row)))
        kw = k * e_col[:, ch:ch + 1]
        dec = decay[:, ch:ch + 1]
        c_ref[h] = dec * c_st + _dot_tn(kw.astype(BF16), vb)
        n_ref[h:h + 1, :] = dec * n_st + jnp.sum(kw, axis=0, keepdims=True)
        lo, hi = h * M_DV, (h + 1) * M_DV
        if reverse:
            hs = hf_ref[0, :, lo:hi] + hout
            mu = jnp.mean(hs, -1, keepdims=True)
            hc = hs - mu
            var = jnp.mean(hc * hc, -1, keepdims=True)
            hn = hc * lax.rsqrt(var + HN_EPS) * hng_ref[:, lo:hi]
            acc_ref[:, lo:hi] = jax.nn.sigmoid(zo_ref[0, :, lo:hi]) * hn
        else:
            out_ref[0, :, lo:hi] = hout
    m_ref[...] = m_new
    if reverse:
        out_ref[0] = acc_ref[...].astype(BF16)


def _mlstm(reverse, zqk, conv_w, zv, gcol, grow, bcol, brow, hfwd=None, zo=None, hng=None):
    b, s, _ = zqk.shape
    L = min(CHUNK, s)
    nc = s // L
    hb = L // SUBLANES
    nhb = s // SUBLANES
    pos = (lambda c: nc - 1 - c) if reverse else (lambda c: c)
    chunk = lambda w: pl.BlockSpec((1, L, w), lambda i, c: (i, pos(c), 0))
    in_specs = [
        chunk(2 * QK_W),
        pl.BlockSpec((1, SUBLANES, 2 * QK_W), lambda i, c: (i, jnp.maximum(pos(c) * hb - 1, 0), 0)),
        pl.BlockSpec((1, SUBLANES, 2 * QK_W), lambda i, c: (i, jnp.minimum((pos(c) + 1) * hb, nhb - 1), 0)),
        pl.BlockSpec((3, 2 * QK_W), lambda i, c: (0, 0)),
        chunk(M_W),
        chunk(GATE_COLS),
        pl.BlockSpec((GATE_COLS, L), lambda i, c: (0, i * nc + pos(c))),
        pl.BlockSpec((1, GATE_COLS), lambda i, c: (0, 0)),
        pl.BlockSpec((GATE_COLS, LANES), lambda i, c: (0, 0)),
    ]
    args = [zqk, zqk, zqk, conv_w, zv, gcol, grow, bcol, brow]
    scratch = [pltpu.VMEM((M_HEADS, M_DK, M_DV), F32), pltpu.VMEM((M_HEADS, M_DK), F32),
               pltpu.VMEM((1, GATE_COLS), F32)]
    if reverse:
        in_specs += [chunk(M_W), chunk(M_W), pl.BlockSpec((1, M_W), lambda i, c: (0, 0))]
        args += [hfwd, zo, hng]
        scratch += [pltpu.VMEM((L, M_W), F32)]
        out_dtype = BF16
    else:
        out_dtype = F32
    return pl.pallas_call(
        functools.partial(_mlstm_kernel, reverse, nc),
        grid=(b, nc),
        in_specs=in_specs,
        out_specs=chunk(M_W),
        out_shape=jax.ShapeDtypeStruct((b, s, M_W), out_dtype),
        scratch_shapes=scratch,
        compiler_params=_cparams(("parallel", "arbitrary"), 48),
        name="mlstm_bwd" if reverse else "mlstm_fwd",
    )(*args)


def _outproj_kernel(four_ref, mo_ref, x_ref, wof_ref, wom_ref, g_ref, b_ref, wr_ref, br_ref,
                    h_ref, route_ref, gate_ref, cnt_ref, carry_ref):
    tm = x_ref.shape[0]

    @pl.when(pl.program_id(0) == 0)
    def _():
        carry_ref[...] = jnp.zeros_like(carry_ref)

    mix = _dot(four_ref[...].astype(BF16), wof_ref[...]) + _dot(mo_ref[...], wom_ref[...])
    h = _layer_norm_rows(ALPHA * x_ref[...] + mix, g_ref[...], b_ref[...])
    h_ref[...] = h
    logits = _dot(h.astype(BF16), wr_ref[...]) + br_ref[...]
    lt = logits.T
    ri = lax.broadcasted_iota(I32, (SUBLANES, tm), 0)
    lc = jnp.where(ri < N_GROUPS, lt[0:SUBLANES], -jnp.inf)
    cmax = jnp.max(lc, axis=0, keepdims=True)
    grp = jnp.min(jnp.where(lc == cmax, ri, SUBLANES), axis=0, keepdims=True)
    p_grp = 1.0 / jnp.sum(jnp.exp(lc - cmax), axis=0, keepdims=True)
    sel = jnp.zeros((EXP_PER_GROUP, tm), F32)
    for g in range(N_GROUPS):
        lo = FINE_OFF + g * EXP_PER_GROUP
        sel = jnp.where(grp == g, lt[lo:lo + EXP_PER_GROUP], sel)
    v1 = jnp.max(sel, axis=0, keepdims=True)
    j1 = jnp.min(jnp.where(sel == v1, ri, SUBLANES), axis=0, keepdims=True)
    rest = jnp.where(ri == j1, -jnp.inf, sel)
    v2 = jnp.max(rest, axis=0, keepdims=True)
    j2 = jnp.min(jnp.where(rest == v2, ri, SUBLANES), axis=0, keepdims=True)
    e21 = jnp.exp(v2 - v1)
    g1 = p_grp / (1.0 + e21)
    g2 = p_grp * e21 / (1.0 + e21)
    eid0 = grp * EXP_PER_GROUP + j1
    eid1 = grp * EXP_PER_GROUP + j2

    ei = lax.broadcasted_iota(I32, (N_EXPERTS, tm), 0)
    oh0 = ei == eid0
    oh1 = ei == eid1
    cnt = jnp.where(oh0 | oh1, 1.0, 0.0).astype(BF16)
    rr = lax.broadcasted_iota(I32, (tm, tm), 0)
    cc = lax.broadcasted_iota(I32, (tm, tm), 1)
    before = jnp.where(rr < cc, 1.0, 0.0).astype(BF16)
    carry = carry_ref[...]
    tot = _dot(cnt, before) + carry[:, 0:1]
    rank0 = jnp.sum(jnp.where(oh0, tot, 0.0), axis=0, keepdims=True)
    rank1 = jnp.sum(jnp.where(oh1, tot, 0.0), axis=0, keepdims=True)
    new_carry = carry + jnp.sum(cnt.astype(F32), axis=1, keepdims=True)
    carry_ref[...] = new_carry
    cnt_ref[...] = new_carry.astype(I32)
    zi = jnp.zeros((SUBLANES - 4, tm), I32)
    route_ref[...] = jnp.concatenate([eid0, eid1, rank0.astype(I32), rank1.astype(I32), zi], axis=0)
    gpad = jnp.concatenate([g1, g2, jnp.zeros((LANES - 2, tm), F32)], axis=0)
    gate_ref[...] = gpad.T


def _outproj(four, mo, x2d, wof, wom, ln_g, ln_b, wr, br):
    t = x2d.shape[0]
    tm = TM_OUT
    row = lambda w: pl.BlockSpec((tm, w), lambda i: (i, 0))
    return pl.pallas_call(
        _outproj_kernel,
        grid=(t // tm,),
        in_specs=[row(F_W), row(M_W), row(D_MODEL), _resident(wof.shape), _resident(wom.shape),
                  _resident((1, D_MODEL)), _resident((1, D_MODEL)), _resident(wr.shape),
                  _resident((1, ROUTE_COLS))],
        out_specs=[row(D_MODEL), pl.BlockSpec((SUBLANES, tm), lambda i: (0, i)), row(LANES),
                   pl.BlockSpec((N_EXPERTS, LANES), lambda i: (0, 0))],
        out_shape=[jax.ShapeDtypeStruct((t, D_MODEL), F32), jax.ShapeDtypeStruct((SUBLANES, t), I32),
                   jax.ShapeDtypeStruct((t, LANES), F32), jax.ShapeDtypeStruct((N_EXPERTS, LANES), I32)],
        scratch_shapes=[pltpu.VMEM((N_EXPERTS, LANES), F32)],
        compiler_params=_cparams(("arbitrary",), 56),
        name="outproj_router",
    )(four, mo, x2d, wof, wom, ln_g, ln_b, wr, br)


def _dest_kernel(route_ref, pstart_ref, dest_ref):
    tm = route_ref.shape[1]
    r = route_ref[...]
    ei = lax.broadcasted_iota(I32, (N_EXPERTS, tm), 0)
    ps = pstart_ref[:, 0:1]
    d0 = jnp.sum(jnp.where(ei == r[0:1], ps, 0), axis=0, keepdims=True) + r[2:3]
    d1 = jnp.sum(jnp.where(ei == r[1:2], ps, 0), axis=0, keepdims=True) + r[3:4]
    dest_ref[...] = jnp.concatenate([d0, d1, jnp.zeros((SUBLANES - 2, tm), I32)], axis=0)


def _dest(route, pstart_b):
    t = route.shape[1]
    tm = 2048
    return pl.pallas_call(
        _dest_kernel,
        grid=(t // tm,),
        in_specs=[pl.BlockSpec((SUBLANES, tm), lambda i: (0, i)),
                  pl.BlockSpec((N_EXPERTS, LANES), lambda i: (0, 0))],
        out_specs=pl.BlockSpec((SUBLANES, tm), lambda i: (0, i)),
        out_shape=jax.ShapeDtypeStruct((SUBLANES, t), I32),
        compiler_params=_cparams(("parallel",), 32),
        name="moe_dest",
    )(route, pstart_b)


def _row_copy(src, dst, sem):
    return pltpu.make_async_copy(src, dst, sem)


def _dispatch_kernel(d0_ref, d1_ref, cnt_ref, ps_ref, nu_ref, h_ref, xs_hbm, zero_ref, sem):
    td = h_ref.shape[0]
    zr = zero_ref.shape[0]
    n_blocks = xs_hbm.shape[0] // MOE_BLK
    base = pl.program_id(0) * td

    def issue(r, c):
        _row_copy(h_ref.at[pl.ds(r, 1), :], xs_hbm.at[pl.ds(d0_ref[base + r], 1), :], sem).start()
        _row_copy(h_ref.at[pl.ds(r, 1), :], xs_hbm.at[pl.ds(d1_ref[base + r], 1), :], sem).start()
        return c

    lax.fori_loop(0, td, issue, 0)

    def drain(r, c):
        _row_copy(h_ref.at[pl.ds(0, 1), :], xs_hbm.at[pl.ds(0, 1), :], sem).wait()
        return c

    lax.fori_loop(0, 2 * td, drain, 0)

    @pl.when(pl.program_id(0) == pl.num_programs(0) - 1)
    def _():
        zero_ref[...] = jnp.zeros_like(zero_ref)

        def per_expert(e, c):
            n = cnt_ref[e]
            npad = (n + MOE_BLK - 1) // MOE_BLK * MOE_BLK - n
            first = ps_ref[e] + n

            def pad_issue(r, c2):
                _row_copy(zero_ref.at[pl.ds(0, 1), :], xs_hbm.at[pl.ds(first + r, 1), :], sem).start()
                return c2

            lax.fori_loop(0, npad, pad_issue, 0)

            def pad_drain(r, c2):
                _row_copy(zero_ref.at[pl.ds(0, 1), :], xs_hbm.at[pl.ds(0, 1), :], sem).wait()
                return c2

            lax.fori_loop(0, npad, pad_drain, 0)
            return c

        lax.fori_loop(0, N_EXPERTS, per_expert, 0)

        def tail_issue(j, c):
            row0 = pl.multiple_of(nu_ref[0] * MOE_BLK + j * zr, zr)
            _row_copy(zero_ref, xs_hbm.at[pl.ds(row0, zr), :], sem).start()
            return c

        n_tail = (n_blocks - nu_ref[0]) * (MOE_BLK // zr)
        lax.fori_loop(0, n_tail, tail_issue, 0)

        def tail_drain(j, c):
            _row_copy(zero_ref, xs_hbm.at[pl.ds(0, zr), :], sem).wait()
            return c

        lax.fori_loop(0, n_tail, tail_drain, 0)


def _dispatch(d0, d1, counts, pstart, n_used, h, n_slots):
    t = h.shape[0]
    return pl.pallas_call(
        _dispatch_kernel,
        grid_spec=pltpu.PrefetchScalarGridSpec(
            num_scalar_prefetch=5, grid=(t // TD,),
            in_specs=[pl.BlockSpec((TD, D_MODEL), lambda i, *_: (i, 0))],
            out_specs=pl.BlockSpec(memory_space=pl.ANY),
            scratch_shapes=[pltpu.VMEM((ZERO_ROWS, D_MODEL), F32), pltpu.SemaphoreType.DMA]),
        out_shape=jax.ShapeDtypeStruct((n_slots, D_MODEL), F32),
        compiler_params=_cparams(("arbitrary",), 32),
        name="moe_dispatch",
    )(d0, d1, counts, pstart, n_used, h)


def _expert_kernel(be_ref, nu_ref, xs_ref, wg_ref, wu_ref, wd_ref, ys_ref):
    i = pl.program_id(0)

    @pl.when(i < nu_ref[0])
    def _():
        xb = xs_ref[...].astype(BF16)
        a = _dot(xb, wg_ref[0])
        u = _dot(xb, wu_ref[0])
        hid = (a * jax.nn.sigmoid(a) * u).astype(BF16)
        ys_ref[...] = _dot(hid, wd_ref[0])

    @pl.when(i >= nu_ref[0])
    def _():
        ys_ref[...] = jnp.zeros_like(ys_ref)


def _experts(blk_e, n_used, xs, wg, wu, wd):
    n_slots = xs.shape[0]
    de = wg.shape[2]
    return pl.pallas_call(
        _expert_kernel,
        grid_spec=pltpu.PrefetchScalarGridSpec(
            num_scalar_prefetch=2, grid=(n_slots // MOE_BLK,),
            in_specs=[pl.BlockSpec((MOE_BLK, D_MODEL), lambda i, be, nu: (jnp.minimum(i, nu[0] - 1), 0)),
                      pl.BlockSpec((1, D_MODEL, de), lambda i, be, nu: (be[i], 0, 0)),
                      pl.BlockSpec((1, D_MODEL, de), lambda i, be, nu: (be[i], 0, 0)),
                      pl.BlockSpec((1, de, D_MODEL), lambda i, be, nu: (be[i], 0, 0))],
            out_specs=pl.BlockSpec((MOE_BLK, D_MODEL), lambda i, be, nu: (i, 0))),
        out_shape=jax.ShapeDtypeStruct((n_slots, D_MODEL), F32),
        compiler_params=_cparams(("arbitrary",), 56),
        name="moe_experts",
    )(blk_e, n_used, xs, wg, wu, wd)


def _combine_kernel(d0_ref, d1_ref, h_ref, gate_ref, ys_hbm, g_ref, b_ref, o_ref, y0_ref, y1_ref, sem):
    td = h_ref.shape[0]
    base = pl.program_id(0) * td

    def issue(r, c):
        _row_copy(ys_hbm.at[pl.ds(d0_ref[base + r], 1), :], y0_ref.at[pl.ds(r, 1), :], sem).start()
        _row_copy(ys_hbm.at[pl.ds(d1_ref[base + r], 1), :], y1_ref.at[pl.ds(r, 1), :], sem).start()
        return c

    lax.fori_loop(0, td, issue, 0)

    def drain(r, c):
        _row_copy(ys_hbm.at[pl.ds(0, 1), :], y0_ref.at[pl.ds(0, 1), :], sem).wait()
        return c

    lax.fori_loop(0, 2 * td, drain, 0)
    gate = gate_ref[...]
    ffn = y0_ref[...] * gate[:, 0:1] + y1_ref[...] * gate[:, 1:2]
    o_ref[...] = _layer_norm_rows(ALPHA * h_ref[...] + ffn, g_ref[...], b_ref[...])


def _combine(d0, d1, h, gate, ys, ln_g, ln_b):
    t = h.shape[0]
    return pl.pallas_call(
        _combine_kernel,
        grid_spec=pltpu.PrefetchScalarGridSpec(
            num_scalar_prefetch=2, grid=(t // TD,),
            in_specs=[pl.BlockSpec((TD, D_MODEL), lambda i, *_: (i, 0)),
                      pl.BlockSpec((TD, LANES), lambda i, *_: (i, 0)),
                      pl.BlockSpec(memory_space=pl.ANY),
                      pl.BlockSpec((1, D_MODEL), lambda i, *_: (0, 0)),
                      pl.BlockSpec((1, D_MODEL), lambda i, *_: (0, 0))],
            out_specs=pl.BlockSpec((TD, D_MODEL), lambda i, *_: (i, 0)),
            scratch_shapes=[pltpu.VMEM((TD, D_MODEL), F32), pltpu.VMEM((TD, D_MODEL), F32),
                            pltpu.SemaphoreType.DMA]),
        out_shape=jax.ShapeDtypeStruct((t, D_MODEL), F32),
        compiler_params=_cparams(("arbitrary",), 32),
        name="moe_combine",
    )(d0, d1, h, gate, ys, ln_g, ln_b)


def _channel_dft_matrix():
    c = np.arange(F_GW, dtype=np.int64)
    ang = 2.0 * np.pi * ((c[:, None] * c[None, :]) % F_GW) / F_GW
    m = np.concatenate([np.cos(ang), -np.sin(ang)], axis=1) / np.sqrt(F_GW)
    return jnp.asarray(m, dtype=BF16)


def _prep_layer(p, l):
    w_in = p["w_in"][l]
    o1 = F_W
    o3 = o1 + 2 * QK_W
    o4 = o3 + M_W
    o5 = o4 + M_W
    n_gate = 4 * M_HEADS
    wg = jnp.pad(w_in[:, o5:], ((0, 0), (0, GATE_COLS - n_gate)))
    bias = jnp.concatenate([p["b_igate"][l].reshape(-1), p["b_fgate"][l].reshape(-1),
                            jnp.zeros((GATE_COLS - n_gate,), F32)])
    wr = jnp.zeros((D_MODEL, ROUTE_COLS), F32)
    wr = wr.at[:, :N_GROUPS].set(p["w_coarse"][l]).at[:, FINE_OFF:FINE_OFF + N_EXPERTS].set(p["w_fine"][l])
    br = jnp.zeros((ROUTE_COLS,), F32)
    br = br.at[:N_GROUPS].set(p["b_coarse"][l]).at[FINE_OFF:FINE_OFF + N_EXPERTS].set(p["b_fine"][l])
    w_out = p["w_out"][l]
    return dict(
        wf=w_in[:, :o1].astype(BF16), wqk=w_in[:, o1:o3].astype(BF16), wv=w_in[:, o3:o4].astype(BF16),
        wo=w_in[:, o4:o5].astype(BF16), wg=wg.astype(BF16),
        conv=p["conv_qk"][l], bcol=bias[None, :], brow=jnp.broadcast_to(bias[:, None], (GATE_COLS, LANES)),
        hng=p["hn_g"][l][None, :],
        wof=w_out[:F_W].astype(BF16), wom=w_out[F_W:].astype(BF16),
        ln1g=p["ln1_g"][l][None, :], ln1b=p["ln1_b"][l][None, :],
        wr=wr.astype(BF16), br=br[None, :],
        weg=p["w_e_gate"][l].astype(BF16), weu=p["w_e_up"][l].astype(BF16), wed=p["w_e_down"][l].astype(BF16),
        ln2g=p["ln2_g"][l][None, :], ln2b=p["ln2_b"][l][None, :],
    )


def _moe(h, route, gate, counts_b, w):
    t = h.shape[0]
    n_blocks = (t * 2) // MOE_BLK + N_EXPERTS
    counts = counts_b[:, 0]
    padded = (counts + MOE_BLK - 1) // MOE_BLK * MOE_BLK
    pend = jnp.cumsum(padded)
    pstart = (pend - padded).astype(I32)
    n_used = (pend[-1] // MOE_BLK).astype(I32)
    blk = jnp.arange(n_blocks, dtype=I32)
    blk_e = jnp.searchsorted(pend, jnp.minimum(blk, n_used - 1) * MOE_BLK, side="right").astype(I32)
    blk_e = jnp.minimum(blk_e, N_EXPERTS - 1)
    dest = _dest(route, jnp.broadcast_to(pstart[:, None], (N_EXPERTS, LANES)))
    d0, d1 = dest[0], dest[1]
    xs = _dispatch(d0, d1, counts.astype(I32), pstart, n_used[None], h, n_blocks * MOE_BLK)
    ys = _experts(blk_e, n_used[None], xs, w["weg"], w["weu"], w["wed"])
    return _combine(d0, d1, h, gate, ys, w["ln2g"], w["ln2b"])


def _encode(x, ln_in_g, ln_in_b, wc, layers):
    b, s, d = x.shape
    t = b * s
    x2d = x.reshape(t, d)
    for l, w in enumerate(layers):
        outs = _inproj(x2d, ln_in_g, ln_in_b, w["wf"], wc, w["wqk"], w["wv"], w["wo"], w["wg"], l == 0)
        if l == 0:
            x2d, outs = outs[0], outs[1:]
        xr, xi, zqk, zv, zo, gcol, grow = outs
        four = _seq_dft(xr.reshape(b, s, F_W), xi.reshape(b, s, F_W)).reshape(t, F_W)
        zqk3 = zqk.reshape(b, s, 2 * QK_W)
        zv3 = zv.reshape(b, s, M_W)
        gcol3 = gcol.reshape(b, s, GATE_COLS)
        hfwd = _mlstm(False, zqk3, w["conv"], zv3, gcol3, grow, w["bcol"], w["brow"])
        mo = _mlstm(True, zqk3, w["conv"], zv3, gcol3, grow, w["bcol"], w["brow"],
                    hfwd, zo.reshape(b, s, M_W), w["hng"])
        h, route, gate, counts_b = _outproj(four, mo.reshape(t, M_W), x2d, w["wof"], w["wom"],
                                            w["ln1g"], w["ln1b"], w["wr"], w["br"])
        x2d = _moe(h, route, gate, counts_b, w)
    return x2d.reshape(b, s, d)


def kernel(x_prompt, x_sample, ln_in_g, ln_in_b, w_in, conv_qk, b_igate, b_fgate, hn_g, w_out, ln1_g, ln1_b,
           w_coarse, b_coarse, w_fine, b_fine, w_e_gate, w_e_up, w_e_down, ln2_g, ln2_b):
    p = dict(w_in=w_in, conv_qk=conv_qk, b_igate=b_igate, b_fgate=b_fgate, hn_g=hn_g, w_out=w_out,
             ln1_g=ln1_g, ln1_b=ln1_b, w_coarse=w_coarse, b_coarse=b_coarse, w_fine=w_fine, b_fine=b_fine,
             w_e_gate=w_e_gate, w_e_up=w_e_up, w_e_down=w_e_down, ln2_g=ln2_g, ln2_b=ln2_b)
    layers = [_prep_layer(p, l) for l in range(w_in.shape[0])]
    wc = _channel_dft_matrix()
    g_in, b_in = ln_in_g[None, :], ln_in_b[None, :]
    y_prompt = _encode(x_prompt, g_in, b_in, wc, layers)
    y_sample = _encode(x_sample, g_in, b_in, wc, layers)
    return (y_prompt, y_sample)
```

```python
import functools

import numpy as np
import jax
import jax.numpy as jnp
from jax import lax
from jax.experimental import pallas as pl
from jax.experimental.pallas import tpu as pltpu

F32 = jnp.float32
BF16 = jnp.bfloat16
I32 = jnp.int32

D_MODEL = 2048
F_W = D_MODEL // 4
F_GROUPS = 4
F_GW = F_W // F_GROUPS
M_W = D_MODEL - F_W
M_HEADS = 8
M_DV = M_W // M_HEADS
M_DK = M_DV // 2
QK_W = M_HEADS * M_DK
N_GROUPS = 4
EXP_PER_GROUP = 8
N_EXPERTS = N_GROUPS * EXP_PER_GROUP
DEPTH = 2
ALPHA = (2 * DEPTH) ** 0.25
LN_EPS = 1e-5
HN_EPS = 1e-6
LOG2E = 1.4426950408889634

LANES = 128
SUBLANES = 8
MIB = 1024 * 1024

TM_IN = 256
TM_OUT = 512
CHUNK = 512
DENSE_DFT_MAX = 4096
TK_DFT = 512
FS2 = 128
FT = 8
MOE_BLK = 512
TD = 256
ZERO_ROWS = 256
ROW_SLABS = D_MODEL // LANES
DMA_UNROLL = 8
GATE_COLS = LANES
ROUTE_COLS = LANES
FINE_OFF = SUBLANES


def _dot(a, b):
    return jnp.dot(a, b, preferred_element_type=F32)


def _dot_nt(a, b):
    return lax.dot_general(a, b, (((1,), (1,)), ((), ())), preferred_element_type=F32)


def _dot_tn(a, b):
    return lax.dot_general(a, b, (((0,), (0,)), ((), ())), preferred_element_type=F32)


def _cparams(sem, vmem_mib):
    return pltpu.CompilerParams(dimension_semantics=sem, vmem_limit_bytes=vmem_mib * MIB)


def _resident(shape):
    nd = len(shape)
    return pl.BlockSpec(shape, lambda *_: (0,) * nd, pipeline_mode=pl.Buffered(1))


def _layer_norm_rows(x, g, b):
    mu = jnp.mean(x, -1, keepdims=True)
    xc = x - mu
    var = jnp.mean(xc * xc, -1, keepdims=True)
    return xc * lax.rsqrt(var + LN_EPS) * g + b


def _log_sigmoid(x):
    return jnp.minimum(x, 0.0) - jnp.log(1.0 + jnp.exp(-jnp.abs(x)))


def _inproj_kernel(apply_ln, x_ref, g_ref, b_ref, wf_ref, wc_ref, wqk_ref, wv_ref, wo_ref, wg_ref, *outs):
    if apply_ln:
        xn_ref, xr_ref, xi_ref, zqk_ref, zv_ref, zo_ref, gcol_ref, grow_ref = outs
    else:
        xr_ref, xi_ref, zqk_ref, zv_ref, zo_ref, gcol_ref, grow_ref = outs
    x = x_ref[...]
    if apply_ln:
        x = _layer_norm_rows(x, g_ref[...], b_ref[...])
        xn_ref[...] = x
    xb = x.astype(BF16)
    zf = _dot(xb, wf_ref[...]).astype(BF16)
    wc = wc_ref[...]
    for g in range(F_GROUPS):
        c = _dot(zf[:, g * F_GW:(g + 1) * F_GW], wc)
        xr_ref[:, g * F_GW:(g + 1) * F_GW] = c[:, :F_GW]
        xi_ref[:, g * F_GW:(g + 1) * F_GW] = c[:, F_GW:]
    zqk_ref[...] = _dot(xb, wqk_ref[...])
    zv_ref[...] = _dot(xb, wv_ref[...]).astype(BF16)
    zo_ref[...] = _dot(xb, wo_ref[...])
    zg = _dot(xb, wg_ref[...])
    gcol_ref[...] = zg
    grow_ref[...] = zg.T


def _inproj(x2d, ln_g, ln_b, wf, wc, wqk, wv, wo, wg, apply_ln):
    t = x2d.shape[0]
    tm = TM_IN
    row = lambda w: pl.BlockSpec((tm, w), lambda i: (i, 0))
    out_shape = [
        jax.ShapeDtypeStruct((t, F_W), F32), jax.ShapeDtypeStruct((t, F_W), F32),
        jax.ShapeDtypeStruct((t, 2 * QK_W), F32), jax.ShapeDtypeStruct((t, M_W), BF16),
        jax.ShapeDtypeStruct((t, M_W), F32), jax.ShapeDtypeStruct((t, GATE_COLS), F32),
        jax.ShapeDtypeStruct((GATE_COLS, t), F32),
    ]
    out_specs = [row(F_W), row(F_W), row(2 * QK_W), row(M_W), row(M_W), row(GATE_COLS),
                 pl.BlockSpec((GATE_COLS, tm), lambda i: (0, i))]
    if apply_ln:
        out_shape = [jax.ShapeDtypeStruct((t, D_MODEL), F32)] + out_shape
        out_specs = [row(D_MODEL)] + out_specs
    return pl.pallas_call(
        functools.partial(_inproj_kernel, apply_ln),
        grid=(t // tm,),
        in_specs=[row(D_MODEL), _resident((1, D_MODEL)), _resident((1, D_MODEL)),
                  _resident(wf.shape), _resident(wc.shape), _resident(wqk.shape),
                  _resident(wv.shape), _resident(wo.shape), _resident(wg.shape)],
        out_specs=out_specs, out_shape=out_shape,
        compiler_params=_cparams(("parallel",), 56),
        name="inproj",
    )(x2d, ln_g, ln_b, wf, wc, wqk, wv, wo, wg)


def _dft_dense_kernel(fc_ref, fs_ref, xr_ref, xi_ref, o_ref):
    o_ref[0] = (_dot(fc_ref[...], xr_ref[0].astype(BF16))
                + _dot(fs_ref[...], xi_ref[0].astype(BF16)))


def _dft_dense(xr, xi):
    b, s, w = xr.shape
    tk = min(TK_DFT, s)
    k = jnp.arange(s, dtype=I32)
    ang = (2.0 * np.pi / s) * ((k[:, None] * k[None, :]) % s).astype(F32)
    scale = 1.0 / np.sqrt(s)
    fc = (jnp.cos(ang) * scale).astype(BF16)
    fs = (jnp.sin(ang) * scale).astype(BF16)
    return pl.pallas_call(
        _dft_dense_kernel,
        grid=(b, s // tk),
        in_specs=[pl.BlockSpec((tk, s), lambda i, j: (j, 0)), pl.BlockSpec((tk, s), lambda i, j: (j, 0)),
                  pl.BlockSpec((1, s, w), lambda i, j: (i, 0, 0)), pl.BlockSpec((1, s, w), lambda i, j: (i, 0, 0))],
        out_specs=pl.BlockSpec((1, tk, w), lambda i, j: (i, j, 0)),
        out_shape=jax.ShapeDtypeStruct((b, s, w), F32),
        compiler_params=_cparams(("parallel", "arbitrary"), 48),
        name="dft_dense",
    )(fc, fs, xr, xi)


def _dft_stage1_kernel(f_ref, xr_ref, xi_ref, ur_ref, ui_ref):
    s1 = xr_ref.shape[1]
    f = f_ref[...]
    for j in range(FT):
        d = jnp.concatenate([xr_ref[0, :, j, :], xi_ref[0, :, j, :]], axis=0).astype(BF16)
        u = _dot(f, d)
        ur_ref[0, :, j, :] = u[:s1]
        ui_ref[0, :, j, :] = u[s1:]


def _dft_stage2_kernel(t_ref, ur_ref, ui_ref, o_ref):
    for j in range(FT):
        d = jnp.concatenate([ur_ref[0, j], ui_ref[0, j]], axis=0).astype(BF16)
        o_ref[0, :, j, :] = _dot(t_ref[j], d)


def _dft_two_stage(xr, xi):
    b, s, w = xr.shape
    s1 = s // FS2
    assert s1 * FS2 == s and s1 % FT == 0 and FS2 % FT == 0
    xr4 = xr.reshape(b, s1, FS2, w)
    xi4 = xi.reshape(b, s1, FS2, w)
    k1 = np.arange(s1, dtype=np.int64)
    a1 = 2.0 * np.pi * ((k1[:, None] * k1[None, :]) % s1) / s1
    c1, sn1 = np.cos(a1), np.sin(a1)
    f1 = jnp.asarray(np.block([[c1, sn1], [-sn1, c1]]) / np.sqrt(s), dtype=BF16)
    s2 = jnp.arange(FS2, dtype=I32)
    kk = jnp.arange(s1, dtype=I32)[:, None] + s1 * jnp.arange(FS2, dtype=I32)[None, :]
    ang = (2.0 * np.pi / s) * ((kk[:, :, None] * s2[None, None, :]) % s).astype(F32)
    t2 = jnp.concatenate([jnp.cos(ang), jnp.sin(ang)], axis=-1).astype(BF16)
    blk1 = pl.BlockSpec((1, s1, FT, w), lambda i, j: (i, 0, j, 0))
    ur, ui = pl.pallas_call(
        _dft_stage1_kernel,
        grid=(b, FS2 // FT),
        in_specs=[pl.BlockSpec((2 * s1, 2 * s1), lambda i, j: (0, 0)), blk1, blk1],
        out_specs=[blk1, blk1],
        out_shape=[jax.ShapeDtypeStruct((b, s1, FS2, w), F32)] * 2,
        compiler_params=_cparams(("parallel", "parallel"), 48),
        name="dft_stage1",
    )(f1, xr4, xi4)
    blk2 = pl.BlockSpec((1, FT, FS2, w), lambda i, j: (i, j, 0, 0))
    y = pl.pallas_call(
        _dft_stage2_kernel,
        grid=(b, s1 // FT),
        in_specs=[pl.BlockSpec((FT, FS2, 2 * FS2), lambda i, j: (j, 0, 0)), blk2, blk2],
        out_specs=pl.BlockSpec((1, FS2, FT, w), lambda i, j: (i, 0, j, 0)),
        out_shape=jax.ShapeDtypeStruct((b, FS2, s1, w), F32),
        compiler_params=_cparams(("parallel", "parallel"), 48),
        name="dft_stage2",
    )(t2, ur, ui)
    return y.reshape(b, s, w)


def _seq_dft(xr, xi):
    if xr.shape[1] <= DENSE_DFT_MAX:
        return _dft_dense(xr, xi)
    return _dft_two_stage(xr, xi)


def _split3(x):
    hi = x.astype(BF16)
    r = x - hi.astype(F32)
    mid = r.astype(BF16)
    lo = (r - mid.astype(F32)).astype(BF16)
    return hi, mid, lo


def _mlstm_kernel(reverse, nc, zqk_ref, hp_ref, hn_ref, cw_ref, zv_ref, gcol_ref, grow_ref,
                  bcol_ref, brow_ref, *rest):
    if reverse:
        hf_ref, zo_ref, hng_ref, out_ref, c_ref, n_ref, m_ref, acc_ref = rest
    else:
        out_ref, c_ref, n_ref, m_ref = rest
    L = zqk_ref.shape[1]
    step = pl.program_id(1)
    cidx = (nc - 1 - step) if reverse else step

    @pl.when(step == 0)
    def _():
        c_ref[...] = jnp.zeros_like(c_ref)
        n_ref[...] = jnp.zeros_like(n_ref)
        m_ref[...] = jnp.zeros_like(m_ref)

    d = 1 if reverse else 0
    i_lo = M_HEADS * d
    f_lo = 2 * M_HEADS + M_HEADS * d

    z = zqk_ref[0]
    rowi = lax.broadcasted_iota(I32, (L, 1), 0)
    prev = jnp.where(cidx > 0, hp_ref[0][SUBLANES - 1:SUBLANES, :], 0.0)
    nxt = jnp.where(cidx < nc - 1, hn_ref[0][0:1, :], 0.0)
    zm1 = jnp.where(rowi == 0, prev, pltpu.roll(z, 1, 0))
    zp1 = jnp.where(rowi == L - 1, nxt, pltpu.roll(z, L - 1, 0))
    cw = cw_ref[...]
    conv = zm1 * cw[0:1] + z * cw[1:2] + zp1 * cw[2:3]
    qk = conv * jax.nn.sigmoid(conv)
    k_all = qk[:, QK_W:]
    qt_all = (qk[:, :QK_W] * (M_DK ** -0.5)).T
    kt_all = k_all.T

    gcol = gcol_ref[0] + bcol_ref[...]
    grow = grow_ref[...] + brow_ref[:, 0:1]
    lf_col = _log_sigmoid(gcol) * LOG2E
    lf_row = _log_sigmoid(grow[f_lo:f_lo + M_HEADS, :]) * LOG2E
    ig_row = grow[i_lo:i_lo + M_HEADS, :] * LOG2E

    si = lax.broadcasted_iota(I32, (L, L), 0)
    ti = lax.broadcasted_iota(I32, (L, L), 1)
    feeds = (si >= ti) if reverse else (si <= ti)
    feeds_b = jnp.where(feeds, 1.0, 0.0).astype(BF16)
    feeds_tb = jnp.where((ti >= si) if reverse else (ti <= si), 1.0, 0.0).astype(BF16)
    a_col = sum(_dot(feeds_tb, p) for p in _split3(lf_col))
    a_row = sum(_dot(p, feeds_b) for p in _split3(lf_row))
    r_col = pltpu.roll(gcol, 2 * M_HEADS, 1) * LOG2E - a_col
    r_row = ig_row - a_row
    edge = 0 if reverse else L - 1
    g_tot = a_row[:, edge:edge + 1]
    m_st = m_ref[:, 0:1]
    lane = lax.broadcasted_iota(I32, (1, L), 1)
    run = r_row
    k = 1
    while k < L:
        if reverse:
            shifted = jnp.where(lane < L - k, pltpu.roll(run, L - k, 1), -jnp.inf)
        else:
            shifted = jnp.where(lane >= k, pltpu.roll(run, k, 1), -jnp.inf)
        run = jnp.maximum(run, shifted)
        k *= 2
    u_row = jnp.maximum(m_st, run)
    w_inter_all = jnp.exp2(m_st - u_row)
    floor_all = jnp.exp2(-(a_row + u_row))
    wl_row = g_tot - a_row + ig_row
    m_new = jnp.maximum(g_tot + m_st, jnp.max(wl_row, axis=1, keepdims=True))
    decay = jnp.exp2(g_tot + m_st - m_new)
    e_row = jnp.exp2(wl_row - m_new)

    zv = zv_ref[0]
    for h in range(M_HEADS):
        ch = f_lo + h
        kb = k_all[:, h * M_DK:(h + 1) * M_DK].astype(BF16)
        qt = qt_all[h * M_DK:(h + 1) * M_DK, :]
        kt = kt_all[h * M_DK:(h + 1) * M_DK, :]
        vb = zv[:, h * M_DV:(h + 1) * M_DV]
        expo = jnp.where(feeds, r_col[:, ch:ch + 1] - u_row[h:h + 1, :], -jnp.inf)
        s_t = _dot(kb, qt.astype(BF16)) * jnp.exp2(expo)
        w_inter = w_inter_all[h:h + 1, :]
        c_st = c_ref[h]
        n_st = n_ref[h]
        den = (jnp.sum(s_t, axis=0, keepdims=True)
               + w_inter * jnp.sum(qt * n_st, axis=0, keepdims=True))
        inv = 1.0 / jnp.maximum(jnp.abs(den), floor_all[h:h + 1, :])
        lhs = jnp.concatenate([(s_t * inv).astype(BF16), (qt * (w_inter * inv)).astype(BF16)], axis=0)
        rhs = jnp.concatenate([vb, c_st.astype(BF16)], axis=0)
        hout = _dot_tn(lhs, rhs)
        kw_t = kt * e_row[h:h + 1, :]
        dec = decay[h:h + 1, :]
        c_ref[h] = dec * c_st + _dot(kw_t.astype(BF16), vb)
        n_ref[h] = dec * n_st + jnp.sum(kw_t, axis=1, keepdims=True)
        lo, hi = h * M_DV, (h + 1) * M_DV
        if reverse:
            acc_ref[:, lo:hi] = hout
        else:
            out_ref[0, :, lo:hi] = hout
    m_ref[...] = jnp.broadcast_to(m_new, m_ref.shape)
    if reverse:
        pw = 2 * M_DV
        first = lax.broadcasted_iota(I32, (1, pw), 1) < M_DV
        for p in range(M_HEADS // 2):
            lo, hi = p * pw, (p + 1) * pw
            hs = acc_ref[:, lo:hi] + hf_ref[0, :, lo:hi]
            s_a = jnp.sum(jnp.where(first, hs, 0.0), -1, keepdims=True)
            s_b = jnp.sum(jnp.where(first, 0.0, hs), -1, keepdims=True)
            hc = hs - jnp.where(first, s_a, s_b) * (1.0 / M_DV)
            sq = hc * hc
            v_a = jnp.sum(jnp.where(first, sq, 0.0), -1, keepdims=True)
            v_b = jnp.sum(jnp.where(first, 0.0, sq), -1, keepdims=True)
            rstd = jnp.where(first, lax.rsqrt(v_a * (1.0 / M_DV) + HN_EPS), lax.rsqrt(v_b * (1.0 / M_DV) + HN_EPS))
            y = hc * rstd * hng_ref[:, lo:hi] * jax.nn.sigmoid(zo_ref[0, :, lo:hi])
            out_ref[0, :, lo:hi] = y.astype(BF16)


def _mlstm(reverse, zqk, conv_w, zv, gcol, grow, bcol, brow, hfwd=None, zo=None, hng=None):
    b, s, _ = zqk.shape
    L = min(CHUNK, s)
    nc = s // L
    hb = L // SUBLANES
    nhb = s // SUBLANES
    pos = (lambda c: nc - 1 - c) if reverse else (lambda c: c)
    chunk = lambda w: pl.BlockSpec((1, L, w), lambda i, c: (i, pos(c), 0))
    in_specs = [
        chunk(2 * QK_W),
        pl.BlockSpec((1, SUBLANES, 2 * QK_W), lambda i, c: (i, jnp.maximum(pos(c) * hb - 1, 0), 0)),
        pl.BlockSpec((1, SUBLANES, 2 * QK_W), lambda i, c: (i, jnp.minimum((pos(c) + 1) * hb, nhb - 1), 0)),
        pl.BlockSpec((3, 2 * QK_W), lambda i, c: (0, 0)),
        chunk(M_W),
        chunk(GATE_COLS),
        pl.BlockSpec((GATE_COLS, L), lambda i, c: (0, i * nc + pos(c))),
        pl.BlockSpec((1, GATE_COLS), lambda i, c: (0, 0)),
        pl.BlockSpec((GATE_COLS, LANES), lambda i, c: (0, 0)),
    ]
    args = [zqk, zqk, zqk, conv_w, zv, gcol, grow, bcol, brow]
    scratch = [pltpu.VMEM((M_HEADS, M_DK, M_DV), F32), pltpu.VMEM((M_HEADS, M_DK, 1), F32),
               pltpu.VMEM((M_HEADS, LANES), F32)]
    if reverse:
        in_specs += [chunk(M_W), chunk(M_W), pl.BlockSpec((1, M_W), lambda i, c: (0, 0))]
        args += [hfwd, zo, hng]
        scratch += [pltpu.VMEM((L, M_W), F32)]
        out_dtype = BF16
    else:
        out_dtype = F32
    return pl.pallas_call(
        functools.partial(_mlstm_kernel, reverse, nc),
        grid=(b, nc),
        in_specs=in_specs,
        out_specs=chunk(M_W),
        out_shape=jax.ShapeDtypeStruct((b, s, M_W), out_dtype),
        scratch_shapes=scratch,
        compiler_params=_cparams(("parallel", "arbitrary"), 48),
        name="mlstm_bwd" if reverse else "mlstm_fwd",
    )(*args)


def _outproj_kernel(four_ref, mo_ref, x_ref, wof_ref, wom_ref, g_ref, b_ref, wr_ref, br_ref,
                    h_ref, route_ref, gate_ref, cnt_ref, carry_ref):
    tm = x_ref.shape[0]

    @pl.when(pl.program_id(0) == 0)
    def _():
        carry_ref[...] = jnp.zeros_like(carry_ref)

    mix = _dot(four_ref[...].astype(BF16), wof_ref[...]) + _dot(mo_ref[...], wom_ref[...])
    h = _layer_norm_rows(ALPHA * x_ref[...] + mix, g_ref[...], b_ref[...])
    h_ref[...] = h
    logits = _dot(h.astype(BF16), wr_ref[...]) + br_ref[...]
    lt = logits.T
    ri = lax.broadcasted_iota(I32, (SUBLANES, tm), 0)
    lc = jnp.where(ri < N_GROUPS, lt[0:SUBLANES], -jnp.inf)
    cmax = jnp.max(lc, axis=0, keepdims=True)
    grp = jnp.min(jnp.where(lc == cmax, ri, SUBLANES), axis=0, keepdims=True)
    p_grp = 1.0 / jnp.sum(jnp.exp(lc - cmax), axis=0, keepdims=True)
    sel = jnp.zeros((EXP_PER_GROUP, tm), F32)
    for g in range(N_GROUPS):
        lo = FINE_OFF + g * EXP_PER_GROUP
        sel = jnp.where(grp == g, lt[lo:lo + EXP_PER_GROUP], sel)
    v1 = jnp.max(sel, axis=0, keepdims=True)
    j1 = jnp.min(jnp.where(sel == v1, ri, SUBLANES), axis=0, keepdims=True)
    rest = jnp.where(ri == j1, -jnp.inf, sel)
    v2 = jnp.max(rest, axis=0, keepdims=True)
    j2 = jnp.min(jnp.where(rest == v2, ri, SUBLANES), axis=0, keepdims=True)
    e21 = jnp.exp(v2 - v1)
    g1 = p_grp / (1.0 + e21)
    g2 = p_grp * e21 / (1.0 + e21)
    eid0 = grp * EXP_PER_GROUP + j1
    eid1 = grp * EXP_PER_GROUP + j2

    ei = lax.broadcasted_iota(I32, (N_EXPERTS, tm), 0)
    oh0 = ei == eid0
    oh1 = ei == eid1
    cnt = jnp.where(oh0 | oh1, 1.0, 0.0).astype(BF16)
    rr = lax.broadcasted_iota(I32, (tm, tm), 0)
    cc = lax.broadcasted_iota(I32, (tm, tm), 1)
    before = jnp.where(rr < cc, 1.0, 0.0).astype(BF16)
    carry = carry_ref[...]
    tot = _dot(cnt, before) + carry[:, 0:1]
    rank0 = jnp.sum(jnp.where(oh0, tot, 0.0), axis=0, keepdims=True)
    rank1 = jnp.sum(jnp.where(oh1, tot, 0.0), axis=0, keepdims=True)
    new_carry = carry + jnp.sum(cnt.astype(F32), axis=1, keepdims=True)
    carry_ref[...] = new_carry
    cnt_ref[...] = new_carry.astype(I32)
    zi = jnp.zeros((SUBLANES - 4, tm), I32)
    route_ref[...] = jnp.concatenate([eid0, eid1, rank0.astype(I32), rank1.astype(I32), zi], axis=0)
    gpad = jnp.concatenate([g1, g2, jnp.zeros((LANES - 2, tm), F32)], axis=0)
    gate_ref[...] = gpad.T


def _outproj(four, mo, x2d, wof, wom, ln_g, ln_b, wr, br):
    t = x2d.shape[0]
    tm = TM_OUT
    row = lambda w: pl.BlockSpec((tm, w), lambda i: (i, 0))
    return pl.pallas_call(
        _outproj_kernel,
        grid=(t // tm,),
        in_specs=[row(F_W), row(M_W), row(D_MODEL), _resident(wof.shape), _resident(wom.shape),
                  _resident((1, D_MODEL)), _resident((1, D_MODEL)), _resident(wr.shape),
                  _resident((1, ROUTE_COLS))],
        out_specs=[row(D_MODEL), pl.BlockSpec((SUBLANES, tm), lambda i: (0, i)), row(LANES),
                   pl.BlockSpec((N_EXPERTS, LANES), lambda i: (0, 0))],
        out_shape=[jax.ShapeDtypeStruct((t, D_MODEL), F32), jax.ShapeDtypeStruct((SUBLANES, t), I32),
                   jax.ShapeDtypeStruct((t, LANES), F32), jax.ShapeDtypeStruct((N_EXPERTS, LANES), I32)],
        scratch_shapes=[pltpu.VMEM((N_EXPERTS, LANES), F32)],
        compiler_params=_cparams(("arbitrary",), 56),
        name="outproj_router",
    )(four, mo, x2d, wof, wom, ln_g, ln_b, wr, br)


def _dest_kernel(route_ref, pstart_ref, dest_ref):
    tm = route_ref.shape[1]
    r = route_ref[...]
    ei = lax.broadcasted_iota(I32, (N_EXPERTS, tm), 0)
    ps = pstart_ref[:, 0:1]
    d0 = jnp.sum(jnp.where(ei == r[0:1], ps, 0), axis=0, keepdims=True) + r[2:3]
    d1 = jnp.sum(jnp.where(ei == r[1:2], ps, 0), axis=0, keepdims=True) + r[3:4]
    dest_ref[...] = jnp.concatenate([d0, d1, jnp.zeros((SUBLANES - 2, tm), I32)], axis=0)


def _dest(route, pstart_b):
    t = route.shape[1]
    tm = 2048
    return pl.pallas_call(
        _dest_kernel,
        grid=(t // tm,),
        in_specs=[pl.BlockSpec((SUBLANES, tm), lambda i: (0, i)),
                  pl.BlockSpec((N_EXPERTS, LANES), lambda i: (0, 0))],
        out_specs=pl.BlockSpec((SUBLANES, tm), lambda i: (0, i)),
        out_shape=jax.ShapeDtypeStruct((SUBLANES, t), I32),
        compiler_params=_cparams(("parallel",), 32),
        name="moe_dest",
    )(route, pstart_b)


def _row_copy(src, dst, sem):
    return pltpu.make_async_copy(src, dst, sem)


def _rows_to_slabs(x2d, slab_ref):
    rows = x2d.shape[0]
    for j in range(ROW_SLABS):
        slab_ref[pl.ds(j, rows, stride=ROW_SLABS), :] = x2d[:, j * LANES:(j + 1) * LANES]


def _slabs_to_rows(slab_ref):
    rows = slab_ref.shape[0] // ROW_SLABS
    return jnp.concatenate([slab_ref[pl.ds(j, rows, stride=ROW_SLABS), :] for j in range(ROW_SLABS)], axis=1)


def _slab(ref, r):
    if isinstance(r, int):
        return ref.at[pl.ds(r * ROW_SLABS, ROW_SLABS), :]
    return ref.at[pl.ds(pl.multiple_of(r * ROW_SLABS, ROW_SLABS), ROW_SLABS), :]


def _dispatch_kernel(d0_ref, d1_ref, cnt_ref, ps_ref, nu_ref, h_ref, xs_hbm, rows_ref, zero_ref, sem):
    td = h_ref.shape[0]
    zr = zero_ref.shape[0] // ROW_SLABS
    n_blocks = xs_hbm.shape[0] // (MOE_BLK * ROW_SLABS)
    base = pl.program_id(0) * td
    _rows_to_slabs(h_ref[...], rows_ref)

    def issue(it, c):
        for u in range(DMA_UNROLL):
            r = it * DMA_UNROLL + u
            _row_copy(_slab(rows_ref, r), _slab(xs_hbm, d0_ref[base + r]), sem).start()
            _row_copy(_slab(rows_ref, r), _slab(xs_hbm, d1_ref[base + r]), sem).start()
        return c

    lax.fori_loop(0, td // DMA_UNROLL, issue, 0)

    def drain(it, c):
        for _ in range(2 * DMA_UNROLL):
            _row_copy(_slab(rows_ref, 0), _slab(xs_hbm, 0), sem).wait()
        return c

    lax.fori_loop(0, td // DMA_UNROLL, drain, 0)

    @pl.when(pl.program_id(0) == pl.num_programs(0) - 1)
    def _():
        zero_ref[...] = jnp.zeros_like(zero_ref)

        def per_expert(e, c):
            n = cnt_ref[e]
            npad = (n + MOE_BLK - 1) // MOE_BLK * MOE_BLK - n
            first = ps_ref[e] + n

            def pad_issue(r, c2):
                _row_copy(_slab(zero_ref, 0), _slab(xs_hbm, first + r), sem).start()
                return c2

            lax.fori_loop(0, npad, pad_issue, 0)

            def pad_drain(r, c2):
                _row_copy(_slab(zero_ref, 0), _slab(xs_hbm, 0), sem).wait()
                return c2

            lax.fori_loop(0, npad, pad_drain, 0)
            return c

        lax.fori_loop(0, N_EXPERTS, per_expert, 0)

        def tail_issue(j, c):
            row0 = pl.multiple_of((nu_ref[0] * MOE_BLK + j * zr) * ROW_SLABS, zr * ROW_SLABS)
            _row_copy(zero_ref, xs_hbm.at[pl.ds(row0, zr * ROW_SLABS), :], sem).start()
            return c

        n_tail = (n_blocks - nu_ref[0]) * (MOE_BLK // zr)
        lax.fori_loop(0, n_tail, tail_issue, 0)

        def tail_drain(j, c):
            _row_copy(zero_ref, xs_hbm.at[pl.ds(0, zr * ROW_SLABS), :], sem).wait()
            return c

        lax.fori_loop(0, n_tail, tail_drain, 0)


def _dispatch(d0, d1, counts, pstart, n_used, h, n_slots):
    t = h.shape[0]
    return pl.pallas_call(
        _dispatch_kernel,
        grid_spec=pltpu.PrefetchScalarGridSpec(
            num_scalar_prefetch=5, grid=(t // TD,),
            in_specs=[pl.BlockSpec((TD, D_MODEL), lambda i, *_: (i, 0))],
            out_specs=pl.BlockSpec(memory_space=pl.ANY),
            scratch_shapes=[pltpu.VMEM((TD * ROW_SLABS, LANES), F32), pltpu.VMEM((ZERO_ROWS * ROW_SLABS, LANES), F32),
                            pltpu.SemaphoreType.DMA]),
        out_shape=jax.ShapeDtypeStruct((n_slots * ROW_SLABS, LANES), F32),
        compiler_params=_cparams(("arbitrary",), 32),
        name="moe_dispatch",
    )(d0, d1, counts, pstart, n_used, h)


def _expert_kernel(be_ref, nu_ref, xs_ref, wg_ref, wu_ref, wd_ref, ys_ref):
    i = pl.program_id(0)

    @pl.when(i < nu_ref[0])
    def _():
        xb = _slabs_to_rows(xs_ref).astype(BF16)
        a = _dot(xb, wg_ref[0, 0])
        u = _dot(xb, wu_ref[0, 0])
        hid = (a * jax.nn.sigmoid(a) * u).astype(BF16)
        _rows_to_slabs(_dot(hid, wd_ref[0, 0]), ys_ref)

    @pl.when(i >= nu_ref[0])
    def _():
        ys_ref[...] = jnp.zeros_like(ys_ref)


def _experts(blk_e, n_used, xs, wg, wu, wd, layer):
    n_slots = xs.shape[0] // ROW_SLABS
    de = wg.shape[3]
    rows = pl.BlockSpec((MOE_BLK * ROW_SLABS, LANES), lambda i, be, nu: (jnp.minimum(i, nu[0] - 1), 0))
    return pl.pallas_call(
        _expert_kernel,
        grid_spec=pltpu.PrefetchScalarGridSpec(
            num_scalar_prefetch=2, grid=(n_slots // MOE_BLK,),
            in_specs=[rows,
                      pl.BlockSpec((1, 1, D_MODEL, de), lambda i, be, nu: (layer, be[i], 0, 0)),
                      pl.BlockSpec((1, 1, D_MODEL, de), lambda i, be, nu: (layer, be[i], 0, 0)),
                      pl.BlockSpec((1, 1, de, D_MODEL), lambda i, be, nu: (layer, be[i], 0, 0))],
            out_specs=pl.BlockSpec((MOE_BLK * ROW_SLABS, LANES), lambda i, be, nu: (i, 0))),
        out_shape=jax.ShapeDtypeStruct((n_slots * ROW_SLABS, LANES), F32),
        compiler_params=_cparams(("arbitrary",), 56),
        name="moe_experts",
    )(blk_e, n_used, xs, wg, wu, wd)


def _combine_kernel(d0_ref, d1_ref, h_ref, gate_ref, ys_hbm, g_ref, b_ref, o_ref, y0_ref, y1_ref, sem):
    td = h_ref.shape[0]
    step = pl.program_id(0)
    slot = step % 2

    def issue(tile, buf):
        base = tile * td

        def body(it, c):
            for u in range(DMA_UNROLL):
                r = it * DMA_UNROLL + u
                _row_copy(_slab(ys_hbm, d0_ref[base + r]), _slab(y0_ref.at[buf], r), sem.at[buf]).start()
                _row_copy(_slab(ys_hbm, d1_ref[base + r]), _slab(y1_ref.at[buf], r), sem.at[buf]).start()
            return c

        lax.fori_loop(0, td // DMA_UNROLL, body, 0)

    @pl.when(step == 0)
    def _():
        issue(0, 0)

    @pl.when(step + 1 < pl.num_programs(0))
    def _():
        issue(step + 1, 1 - slot)

    def drain(it, c):
        for _ in range(2 * DMA_UNROLL):
            _row_copy(_slab(ys_hbm, 0), _slab(y0_ref.at[slot], 0), sem.at[slot]).wait()
        return c

    lax.fori_loop(0, td // DMA_UNROLL, drain, 0)
    gate = gate_ref[...]
    ffn = _slabs_to_rows(y0_ref.at[slot]) * gate[:, 0:1] + _slabs_to_rows(y1_ref.at[slot]) * gate[:, 1:2]
    o_ref[...] = _layer_norm_rows(ALPHA * h_ref[...] + ffn, g_ref[...], b_ref[...])


def _combine(d0, d1, h, gate, ys, ln_g, ln_b):
    t = h.shape[0]
    return pl.pallas_call(
        _combine_kernel,
        grid_spec=pltpu.PrefetchScalarGridSpec(
            num_scalar_prefetch=2, grid=(t // TD,),
            in_specs=[pl.BlockSpec((TD, D_MODEL), lambda i, *_: (i, 0)),
                      pl.BlockSpec((TD, LANES), lambda i, *_: (i, 0)),
                      pl.BlockSpec(memory_space=pl.ANY),
                      pl.BlockSpec((1, D_MODEL), lambda i, *_: (0, 0)),
                      pl.BlockSpec((1, D_MODEL), lambda i, *_: (0, 0))],
            out_specs=pl.BlockSpec((TD, D_MODEL), lambda i, *_: (i, 0)),
            scratch_shapes=[pltpu.VMEM((2, TD * ROW_SLABS, LANES), F32), pltpu.VMEM((2, TD * ROW_SLABS, LANES), F32),
                            pltpu.SemaphoreType.DMA((2,))]),
        out_shape=jax.ShapeDtypeStruct((t, D_MODEL), F32),
        compiler_params=_cparams(("arbitrary",), 32),
        name="moe_combine",
    )(d0, d1, h, gate, ys, ln_g, ln_b)


def _channel_dft_matrix():
    c = np.arange(F_GW, dtype=np.int64)
    ang = 2.0 * np.pi * ((c[:, None] * c[None, :]) % F_GW) / F_GW
    m = np.concatenate([np.cos(ang), -np.sin(ang)], axis=1) / np.sqrt(F_GW)
    return jnp.asarray(m, dtype=BF16)


def _prep_layer(p, l):
    w_in = p["w_in"][l]
    o1 = F_W
    o3 = o1 + 2 * QK_W
    o4 = o3 + M_W
    o5 = o4 + M_W
    n_gate = 4 * M_HEADS
    wg = jnp.pad(w_in[:, o5:], ((0, 0), (0, GATE_COLS - n_gate)))
    bias = jnp.concatenate([p["b_igate"][l].reshape(-1), p["b_fgate"][l].reshape(-1),
                            jnp.zeros((GATE_COLS - n_gate,), F32)])
    wr = jnp.zeros((D_MODEL, ROUTE_COLS), F32)
    wr = wr.at[:, :N_GROUPS].set(p["w_coarse"][l]).at[:, FINE_OFF:FINE_OFF + N_EXPERTS].set(p["w_fine"][l])
    br = jnp.zeros((ROUTE_COLS,), F32)
    br = br.at[:N_GROUPS].set(p["b_coarse"][l]).at[FINE_OFF:FINE_OFF + N_EXPERTS].set(p["b_fine"][l])
    w_out = p["w_out"][l]
    return dict(
        wf=w_in[:, :o1].astype(BF16), wqk=w_in[:, o1:o3].astype(BF16), wv=w_in[:, o3:o4].astype(BF16),
        wo=w_in[:, o4:o5].astype(BF16), wg=wg.astype(BF16),
        conv=p["conv_qk"][l], bcol=bias[None, :], brow=jnp.broadcast_to(bias[:, None], (GATE_COLS, LANES)),
        hng=p["hn_g"][l][None, :],
        wof=w_out[:F_W].astype(BF16), wom=w_out[F_W:].astype(BF16),
        ln1g=p["ln1_g"][l][None, :], ln1b=p["ln1_b"][l][None, :],
        wr=wr.astype(BF16), br=br[None, :],
        weg=p["weg"], weu=p["weu"], wed=p["wed"], layer=l,
        ln2g=p["ln2_g"][l][None, :], ln2b=p["ln2_b"][l][None, :],
    )


def _moe(h, route, gate, counts_b, w):
    t = h.shape[0]
    n_blocks = (t * 2) // MOE_BLK + N_EXPERTS
    counts = counts_b[:, 0]
    padded = (counts + MOE_BLK - 1) // MOE_BLK * MOE_BLK
    pend = jnp.cumsum(padded)
    pstart = (pend - padded).astype(I32)
    n_used = (pend[-1] // MOE_BLK).astype(I32)
    first_row = jnp.minimum(jnp.arange(n_blocks, dtype=I32), n_used - 1) * MOE_BLK
    blk_e = jnp.sum((pend[None, :] <= first_row[:, None]).astype(I32), axis=1)
    blk_e = jnp.minimum(blk_e, N_EXPERTS - 1)
    dest = _dest(route, jnp.broadcast_to(pstart[:, None], (N_EXPERTS, LANES)))
    d0, d1 = dest[0], dest[1]
    xs = _dispatch(d0, d1, counts.astype(I32), pstart, n_used[None], h, n_blocks * MOE_BLK)
    ys = _experts(blk_e, n_used[None], xs, w["weg"], w["weu"], w["wed"], w["layer"])
    return _combine(d0, d1, h, gate, ys, w["ln2g"], w["ln2b"])


def _encode(x, ln_in_g, ln_in_b, wc, layers):
    b, s, d = x.shape
    t = b * s
    x2d = x.reshape(t, d)
    for l, w in enumerate(layers):
        outs = _inproj(x2d, ln_in_g, ln_in_b, w["wf"], wc, w["wqk"], w["wv"], w["wo"], w["wg"], l == 0)
        if l == 0:
            x2d, outs = outs[0], outs[1:]
        xr, xi, zqk, zv, zo, gcol, grow = outs
        four = _seq_dft(xr.reshape(b, s, F_W), xi.reshape(b, s, F_W)).reshape(t, F_W)
        zqk3 = zqk.reshape(b, s, 2 * QK_W)
        zv3 = zv.reshape(b, s, M_W)
        gcol3 = gcol.reshape(b, s, GATE_COLS)
        hfwd = _mlstm(False, zqk3, w["conv"], zv3, gcol3, grow, w["bcol"], w["brow"])
        mo = _mlstm(True, zqk3, w["conv"], zv3, gcol3, grow, w["bcol"], w["brow"],
                    hfwd, zo.reshape(b, s, M_W), w["hng"])
        h, route, gate, counts_b = _outproj(four, mo.reshape(t, M_W), x2d, w["wof"], w["wom"],
                                            w["ln1g"], w["ln1b"], w["wr"], w["br"])
        x2d = _moe(h, route, gate, counts_b, w)
    return x2d.reshape(b, s, d)


def kernel(x_prompt, x_sample, ln_in_g, ln_in_b, w_in, conv_qk, b_igate, b_fgate, hn_g, w_out, ln1_g, ln1_b,
           w_coarse, b_coarse, w_fine, b_fine, w_e_gate, w_e_up, w_e_down, ln2_g, ln2_b):
    p = dict(w_in=w_in, conv_qk=conv_qk, b_igate=b_igate, b_fgate=b_fgate, hn_g=hn_g, w_out=w_out,
             ln1_g=ln1_g, ln1_b=ln1_b, w_coarse=w_coarse, b_coarse=b_coarse, w_fine=w_fine, b_fine=b_fine,
             weg=w_e_gate.astype(BF16), weu=w_e_up.astype(BF16), wed=w_e_down.astype(BF16),
             ln2_g=ln2_g, ln2_b=ln2_b)
    layers = [_prep_layer(p, l) for l in range(w_in.shape[0])]
    wc = _channel_dft_matrix()
    g_in, b_in = ln_in_g[None, :], ln_in_b[None, :]
    y_prompt = _encode(x_prompt, g_in, b_in, wc, layers)
    y_sample = _encode(x_sample, g_in, b_in, wc, layers)
    return (y_prompt, y_sample)
```

```python
import functools

import numpy as np
import jax
import jax.numpy as jnp
from jax import lax
from jax.experimental import pallas as pl
from jax.experimental.pallas import tpu as pltpu

F32 = jnp.float32
BF16 = jnp.bfloat16
I32 = jnp.int32
U32 = jnp.uint32

D_MODEL = 2048
F_W = D_MODEL // 4
F_GROUPS = 4
F_GW = F_W // F_GROUPS
M_W = D_MODEL - F_W
M_HEADS = 8
M_DV = M_W // M_HEADS
M_DK = M_DV // 2
QK_W = M_HEADS * M_DK
N_GROUPS = 4
EXP_PER_GROUP = 8
N_EXPERTS = N_GROUPS * EXP_PER_GROUP
DEPTH = 2
ALPHA = (2 * DEPTH) ** 0.25
LN_EPS = 1e-5
HN_EPS = 1e-6
LOG2E = 1.4426950408889634

LANES = 128
SUBLANES = 8
MIB = 1024 * 1024

TM_IN = 256
TM_OUT = 512
CHUNK = 512
DENSE_DFT_MAX = 4096
TK_DFT = 512
FS2 = 128
FT = 8
MOE_BLK = 512
TD = 256
ZERO_ROWS = 256
PACK_W = D_MODEL // 2
ROW_SLABS = PACK_W // LANES
DMA_UNROLL = 8
GATE_COLS = LANES
ROUTE_COLS = LANES
FINE_OFF = SUBLANES


def _dot(a, b):
    return jnp.dot(a, b, preferred_element_type=F32)


def _dot_nt(a, b):
    return lax.dot_general(a, b, (((1,), (1,)), ((), ())), preferred_element_type=F32)


def _dot_tn(a, b):
    return lax.dot_general(a, b, (((0,), (0,)), ((), ())), preferred_element_type=F32)


def _cparams(sem, vmem_mib):
    return pltpu.CompilerParams(dimension_semantics=sem, vmem_limit_bytes=vmem_mib * MIB)


def _resident(shape):
    nd = len(shape)
    return pl.BlockSpec(shape, lambda *_: (0,) * nd, pipeline_mode=pl.Buffered(1))


def _layer_norm_rows(x, g, b):
    mu = jnp.mean(x, -1, keepdims=True)
    xc = x - mu
    var = jnp.mean(xc * xc, -1, keepdims=True)
    return xc * lax.rsqrt(var + LN_EPS) * g + b


def _log_sigmoid(x):
    return jnp.minimum(x, 0.0) - jnp.log(1.0 + jnp.exp(-jnp.abs(x)))


def _inproj_kernel(apply_ln, x_ref, g_ref, b_ref, wf_ref, wc_ref, wqk_ref, wv_ref, wo_ref, wg_ref, *outs):
    if apply_ln:
        xn_ref, xr_ref, xi_ref, zqk_ref, zv_ref, zo_ref, gcol_ref, grow_ref = outs
    else:
        xr_ref, xi_ref, zqk_ref, zv_ref, zo_ref, gcol_ref, grow_ref = outs
    x = x_ref[...]
    if apply_ln:
        x = _layer_norm_rows(x, g_ref[...], b_ref[...])
        xn_ref[...] = x
    xb = x.astype(BF16)
    zf = _dot(xb, wf_ref[...]).astype(BF16)
    wc = wc_ref[...]
    for g in range(F_GROUPS):
        c = _dot(zf[:, g * F_GW:(g + 1) * F_GW], wc)
        xr_ref[:, g * F_GW:(g + 1) * F_GW] = c[:, :F_GW]
        xi_ref[:, g * F_GW:(g + 1) * F_GW] = c[:, F_GW:]
    zqk_ref[...] = _dot(xb, wqk_ref[...])
    zv_ref[...] = _dot(xb, wv_ref[...]).astype(BF16)
    zo_ref[...] = _dot(xb, wo_ref[...])
    zg = _dot(xb, wg_ref[...])
    gcol_ref[...] = zg
    grow_ref[...] = zg.T


def _inproj(x2d, ln_g, ln_b, wf, wc, wqk, wv, wo, wg, apply_ln):
    t = x2d.shape[0]
    tm = TM_IN
    row = lambda w: pl.BlockSpec((tm, w), lambda i: (i, 0))
    out_shape = [
        jax.ShapeDtypeStruct((t, F_W), F32), jax.ShapeDtypeStruct((t, F_W), F32),
        jax.ShapeDtypeStruct((t, 2 * QK_W), F32), jax.ShapeDtypeStruct((t, M_W), BF16),
        jax.ShapeDtypeStruct((t, M_W), F32), jax.ShapeDtypeStruct((t, GATE_COLS), F32),
        jax.ShapeDtypeStruct((GATE_COLS, t), F32),
    ]
    out_specs = [row(F_W), row(F_W), row(2 * QK_W), row(M_W), row(M_W), row(GATE_COLS),
                 pl.BlockSpec((GATE_COLS, tm), lambda i: (0, i))]
    if apply_ln:
        out_shape = [jax.ShapeDtypeStruct((t, D_MODEL), F32)] + out_shape
        out_specs = [row(D_MODEL)] + out_specs
    return pl.pallas_call(
        functools.partial(_inproj_kernel, apply_ln),
        grid=(t // tm,),
        in_specs=[row(D_MODEL), _resident((1, D_MODEL)), _resident((1, D_MODEL)),
                  _resident(wf.shape), _resident(wc.shape), _resident(wqk.shape),
                  _resident(wv.shape), _resident(wo.shape), _resident(wg.shape)],
        out_specs=out_specs, out_shape=out_shape,
        compiler_params=_cparams(("parallel",), 56),
        name="inproj",
    )(x2d, ln_g, ln_b, wf, wc, wqk, wv, wo, wg)


def _dft_dense_kernel(fc_ref, fs_ref, xr_ref, xi_ref, o_ref):
    o_ref[0] = (_dot(fc_ref[...], xr_ref[0].astype(BF16))
                + _dot(fs_ref[...], xi_ref[0].astype(BF16)))


def _dft_dense(xr, xi):
    b, s, w = xr.shape
    tk = min(TK_DFT, s)
    k = jnp.arange(s, dtype=I32)
    ang = (2.0 * np.pi / s) * ((k[:, None] * k[None, :]) % s).astype(F32)
    scale = 1.0 / np.sqrt(s)
    fc = (jnp.cos(ang) * scale).astype(BF16)
    fs = (jnp.sin(ang) * scale).astype(BF16)
    return pl.pallas_call(
        _dft_dense_kernel,
        grid=(b, s // tk),
        in_specs=[pl.BlockSpec((tk, s), lambda i, j: (j, 0)), pl.BlockSpec((tk, s), lambda i, j: (j, 0)),
                  pl.BlockSpec((1, s, w), lambda i, j: (i, 0, 0)), pl.BlockSpec((1, s, w), lambda i, j: (i, 0, 0))],
        out_specs=pl.BlockSpec((1, tk, w), lambda i, j: (i, j, 0)),
        out_shape=jax.ShapeDtypeStruct((b, s, w), F32),
        compiler_params=_cparams(("parallel", "arbitrary"), 48),
        name="dft_dense",
    )(fc, fs, xr, xi)


def _dft_stage1_kernel(f_ref, xr_ref, xi_ref, ur_ref, ui_ref):
    s1 = xr_ref.shape[1]
    f = f_ref[...]
    for j in range(FT):
        d = jnp.concatenate([xr_ref[0, :, j, :], xi_ref[0, :, j, :]], axis=0).astype(BF16)
        u = _dot(f, d)
        ur_ref[0, :, j, :] = u[:s1]
        ui_ref[0, :, j, :] = u[s1:]


def _dft_stage2_kernel(t_ref, ur_ref, ui_ref, o_ref):
    for j in range(FT):
        d = jnp.concatenate([ur_ref[0, j], ui_ref[0, j]], axis=0).astype(BF16)
        o_ref[0, :, j, :] = _dot(t_ref[j], d)


def _dft_two_stage(xr, xi):
    b, s, w = xr.shape
    s1 = s // FS2
    assert s1 * FS2 == s and s1 % FT == 0 and FS2 % FT == 0
    xr4 = xr.reshape(b, s1, FS2, w)
    xi4 = xi.reshape(b, s1, FS2, w)
    k1 = np.arange(s1, dtype=np.int64)
    a1 = 2.0 * np.pi * ((k1[:, None] * k1[None, :]) % s1) / s1
    c1, sn1 = np.cos(a1), np.sin(a1)
    f1 = jnp.asarray(np.block([[c1, sn1], [-sn1, c1]]) / np.sqrt(s), dtype=BF16)
    s2 = jnp.arange(FS2, dtype=I32)
    kk = jnp.arange(s1, dtype=I32)[:, None] + s1 * jnp.arange(FS2, dtype=I32)[None, :]
    ang = (2.0 * np.pi / s) * ((kk[:, :, None] * s2[None, None, :]) % s).astype(F32)
    t2 = jnp.concatenate([jnp.cos(ang), jnp.sin(ang)], axis=-1).astype(BF16)
    blk1 = pl.BlockSpec((1, s1, FT, w), lambda i, j: (i, 0, j, 0))
    ur, ui = pl.pallas_call(
        _dft_stage1_kernel,
        grid=(b, FS2 // FT),
        in_specs=[pl.BlockSpec((2 * s1, 2 * s1), lambda i, j: (0, 0)), blk1, blk1],
        out_specs=[blk1, blk1],
        out_shape=[jax.ShapeDtypeStruct((b, s1, FS2, w), F32)] * 2,
        compiler_params=_cparams(("parallel", "parallel"), 48),
        name="dft_stage1",
    )(f1, xr4, xi4)
    blk2 = pl.BlockSpec((1, FT, FS2, w), lambda i, j: (i, j, 0, 0))
    y = pl.pallas_call(
        _dft_stage2_kernel,
        grid=(b, s1 // FT),
        in_specs=[pl.BlockSpec((FT, FS2, 2 * FS2), lambda i, j: (j, 0, 0)), blk2, blk2],
        out_specs=pl.BlockSpec((1, FS2, FT, w), lambda i, j: (i, 0, j, 0)),
        out_shape=jax.ShapeDtypeStruct((b, FS2, s1, w), F32),
        compiler_params=_cparams(("parallel", "parallel"), 48),
        name="dft_stage2",
    )(t2, ur, ui)
    return y.reshape(b, s, w)


def _seq_dft(xr, xi):
    if xr.shape[1] <= DENSE_DFT_MAX:
        return _dft_dense(xr, xi)
    return _dft_two_stage(xr, xi)


def _split3(x):
    hi = x.astype(BF16)
    r = x - hi.astype(F32)
    mid = r.astype(BF16)
    lo = (r - mid.astype(F32)).astype(BF16)
    return hi, mid, lo


def _mlstm_kernel(reverse, nc, zqk_ref, hp_ref, hn_ref, cw_ref, zv_ref, gcol_ref, grow_ref,
                  bcol_ref, brow_ref, *rest):
    if reverse:
        hf_ref, zo_ref, hng_ref, out_ref, c_ref, n_ref, m_ref, acc_ref = rest
    else:
        out_ref, c_ref, n_ref, m_ref = rest
    L = zqk_ref.shape[1]
    step = pl.program_id(1)
    cidx = (nc - 1 - step) if reverse else step

    @pl.when(step == 0)
    def _():
        c_ref[...] = jnp.zeros_like(c_ref)
        n_ref[...] = jnp.zeros_like(n_ref)
        m_ref[...] = jnp.zeros_like(m_ref)

    d = 1 if reverse else 0
    i_lo = M_HEADS * d
    f_lo = 2 * M_HEADS + M_HEADS * d

    z = zqk_ref[0]
    rowi = lax.broadcasted_iota(I32, (L, 1), 0)
    prev = jnp.where(cidx > 0, hp_ref[0][SUBLANES - 1:SUBLANES, :], 0.0)
    nxt = jnp.where(cidx < nc - 1, hn_ref[0][0:1, :], 0.0)
    zm1 = jnp.where(rowi == 0, prev, pltpu.roll(z, 1, 0))
    zp1 = jnp.where(rowi == L - 1, nxt, pltpu.roll(z, L - 1, 0))
    cw = cw_ref[...]
    conv = zm1 * cw[0:1] + z * cw[1:2] + zp1 * cw[2:3]
    qk = conv * jax.nn.sigmoid(conv)
    k_all = qk[:, QK_W:]
    qt_all = (qk[:, :QK_W] * (M_DK ** -0.5)).T
    kt_all = k_all.T

    gcol = gcol_ref[0] + bcol_ref[...]
    grow = grow_ref[...] + brow_ref[:, 0:1]
    lf_col = _log_sigmoid(gcol) * LOG2E
    lf_row = _log_sigmoid(grow[f_lo:f_lo + M_HEADS, :]) * LOG2E
    ig_row = grow[i_lo:i_lo + M_HEADS, :] * LOG2E

    si = lax.broadcasted_iota(I32, (L, L), 0)
    ti = lax.broadcasted_iota(I32, (L, L), 1)
    feeds = (si >= ti) if reverse else (si <= ti)
    feeds_b = jnp.where(feeds, 1.0, 0.0).astype(BF16)
    feeds_tb = jnp.where((ti >= si) if reverse else (ti <= si), 1.0, 0.0).astype(BF16)
    a_col = sum(_dot(feeds_tb, p) for p in _split3(lf_col))
    a_row = sum(_dot(p, feeds_b) for p in _split3(lf_row))
    r_col = pltpu.roll(gcol, 2 * M_HEADS, 1) * LOG2E - a_col
    r_row = ig_row - a_row
    edge = 0 if reverse else L - 1
    g_tot = a_row[:, edge:edge + 1]
    m_st = m_ref[:, 0:1]
    lane = lax.broadcasted_iota(I32, (1, L), 1)
    run = r_row
    k = 1
    while k < L:
        if reverse:
            shifted = jnp.where(lane < L - k, pltpu.roll(run, L - k, 1), -jnp.inf)
        else:
            shifted = jnp.where(lane >= k, pltpu.roll(run, k, 1), -jnp.inf)
        run = jnp.maximum(run, shifted)
        k *= 2
    u_row = jnp.maximum(m_st, run)
    w_inter_all = jnp.exp2(m_st - u_row)
    floor_all = jnp.exp2(-(a_row + u_row))
    wl_row = g_tot - a_row + ig_row
    m_new = jnp.maximum(g_tot + m_st, jnp.max(wl_row, axis=1, keepdims=True))
    decay = jnp.exp2(g_tot + m_st - m_new)
    e_row = jnp.exp2(wl_row - m_new)

    zv = zv_ref[0]
    for h in range(M_HEADS):
        ch = f_lo + h
        kb = k_all[:, h * M_DK:(h + 1) * M_DK].astype(BF16)
        qt = qt_all[h * M_DK:(h + 1) * M_DK, :]
        kt = kt_all[h * M_DK:(h + 1) * M_DK, :]
        vb = zv[:, h * M_DV:(h + 1) * M_DV]
        expo = jnp.where(feeds, r_col[:, ch:ch + 1] - u_row[h:h + 1, :], -jnp.inf)
        s_t = _dot(kb, qt.astype(BF16)) * jnp.exp2(expo)
        w_inter = w_inter_all[h:h + 1, :]
        c_st = c_ref[h]
        n_st = n_ref[h]
        den = (jnp.sum(s_t, axis=0, keepdims=True)
               + w_inter * jnp.sum(qt * n_st, axis=0, keepdims=True))
        inv = 1.0 / jnp.maximum(jnp.abs(den), floor_all[h:h + 1, :])
        lhs = jnp.concatenate([(s_t * inv).astype(BF16), (qt * (w_inter * inv)).astype(BF16)], axis=0)
        rhs = jnp.concatenate([vb, c_st.astype(BF16)], axis=0)
        hout = _dot_tn(lhs, rhs)
        kw_t = kt * e_row[h:h + 1, :]
        dec = decay[h:h + 1, :]
        c_ref[h] = dec * c_st + _dot(kw_t.astype(BF16), vb)
        n_ref[h] = dec * n_st + jnp.sum(kw_t, axis=1, keepdims=True)
        lo, hi = h * M_DV, (h + 1) * M_DV
        if reverse:
            acc_ref[:, lo:hi] = hout
        else:
            out_ref[0, :, lo:hi] = hout
    m_ref[...] = jnp.broadcast_to(m_new, m_ref.shape)
    if reverse:
        pw = 2 * M_DV
        first = lax.broadcasted_iota(I32, (1, pw), 1) < M_DV
        for p in range(M_HEADS // 2):
            lo, hi = p * pw, (p + 1) * pw
            hs = acc_ref[:, lo:hi] + hf_ref[0, :, lo:hi]
            s_a = jnp.sum(jnp.where(first, hs, 0.0), -1, keepdims=True)
            s_b = jnp.sum(jnp.where(first, 0.0, hs), -1, keepdims=True)
            hc = hs - jnp.where(first, s_a, s_b) * (1.0 / M_DV)
            sq = hc * hc
            v_a = jnp.sum(jnp.where(first, sq, 0.0), -1, keepdims=True)
            v_b = jnp.sum(jnp.where(first, 0.0, sq), -1, keepdims=True)
            rstd = jnp.where(first, lax.rsqrt(v_a * (1.0 / M_DV) + HN_EPS), lax.rsqrt(v_b * (1.0 / M_DV) + HN_EPS))
            y = hc * rstd * hng_ref[:, lo:hi] * jax.nn.sigmoid(zo_ref[0, :, lo:hi])
            out_ref[0, :, lo:hi] = y.astype(BF16)


def _mlstm(reverse, zqk, conv_w, zv, gcol, grow, bcol, brow, hfwd=None, zo=None, hng=None):
    b, s, _ = zqk.shape
    L = min(CHUNK, s)
    nc = s // L
    hb = L // SUBLANES
    nhb = s // SUBLANES
    pos = (lambda c: nc - 1 - c) if reverse else (lambda c: c)
    chunk = lambda w: pl.BlockSpec((1, L, w), lambda i, c: (i, pos(c), 0))
    in_specs = [
        chunk(2 * QK_W),
        pl.BlockSpec((1, SUBLANES, 2 * QK_W), lambda i, c: (i, jnp.maximum(pos(c) * hb - 1, 0), 0)),
        pl.BlockSpec((1, SUBLANES, 2 * QK_W), lambda i, c: (i, jnp.minimum((pos(c) + 1) * hb, nhb - 1), 0)),
        pl.BlockSpec((3, 2 * QK_W), lambda i, c: (0, 0)),
        chunk(M_W),
        chunk(GATE_COLS),
        pl.BlockSpec((GATE_COLS, L), lambda i, c: (0, i * nc + pos(c))),
        pl.BlockSpec((1, GATE_COLS), lambda i, c: (0, 0)),
        pl.BlockSpec((GATE_COLS, LANES), lambda i, c: (0, 0)),
    ]
    args = [zqk, zqk, zqk, conv_w, zv, gcol, grow, bcol, brow]
    scratch = [pltpu.VMEM((M_HEADS, M_DK, M_DV), F32), pltpu.VMEM((M_HEADS, M_DK, 1), F32),
               pltpu.VMEM((M_HEADS, LANES), F32)]
    if reverse:
        in_specs += [chunk(M_W), chunk(M_W), pl.BlockSpec((1, M_W), lambda i, c: (0, 0))]
        args += [hfwd, zo, hng]
        scratch += [pltpu.VMEM((L, M_W), F32)]
        out_dtype = BF16
    else:
        out_dtype = F32
    return pl.pallas_call(
        functools.partial(_mlstm_kernel, reverse, nc),
        grid=(b, nc),
        in_specs=in_specs,
        out_specs=chunk(M_W),
        out_shape=jax.ShapeDtypeStruct((b, s, M_W), out_dtype),
        scratch_shapes=scratch,
        compiler_params=_cparams(("parallel", "arbitrary"), 48),
        name="mlstm_bwd" if reverse else "mlstm_fwd",
    )(*args)


def _outproj_kernel(four_ref, mo_ref, x_ref, wof_ref, wom_ref, g_ref, b_ref, wr_ref, br_ref,
                    h_ref, route_ref, gate_ref, cnt_ref, carry_ref):
    tm = x_ref.shape[0]

    @pl.when(pl.program_id(0) == 0)
    def _():
        carry_ref[...] = jnp.zeros_like(carry_ref)

    mix = _dot(four_ref[...].astype(BF16), wof_ref[...]) + _dot(mo_ref[...], wom_ref[...])
    h = _layer_norm_rows(ALPHA * x_ref[...] + mix, g_ref[...], b_ref[...])
    h_ref[...] = h
    logits = _dot(h.astype(BF16), wr_ref[...]) + br_ref[...]
    lt = logits.T
    ri = lax.broadcasted_iota(I32, (SUBLANES, tm), 0)
    lc = jnp.where(ri < N_GROUPS, lt[0:SUBLANES], -jnp.inf)
    cmax = jnp.max(lc, axis=0, keepdims=True)
    grp = jnp.min(jnp.where(lc == cmax, ri, SUBLANES), axis=0, keepdims=True)
    p_grp = 1.0 / jnp.sum(jnp.exp(lc - cmax), axis=0, keepdims=True)
    sel = jnp.zeros((EXP_PER_GROUP, tm), F32)
    for g in range(N_GROUPS):
        lo = FINE_OFF + g * EXP_PER_GROUP
        sel = jnp.where(grp == g, lt[lo:lo + EXP_PER_GROUP], sel)
    v1 = jnp.max(sel, axis=0, keepdims=True)
    j1 = jnp.min(jnp.where(sel == v1, ri, SUBLANES), axis=0, keepdims=True)
    rest = jnp.where(ri == j1, -jnp.inf, sel)
    v2 = jnp.max(rest, axis=0, keepdims=True)
    j2 = jnp.min(jnp.where(rest == v2, ri, SUBLANES), axis=0, keepdims=True)
    e21 = jnp.exp(v2 - v1)
    g1 = p_grp / (1.0 + e21)
    g2 = p_grp * e21 / (1.0 + e21)
    eid0 = grp * EXP_PER_GROUP + j1
    eid1 = grp * EXP_PER_GROUP + j2

    ei = lax.broadcasted_iota(I32, (N_EXPERTS, tm), 0)
    oh0 = ei == eid0
    oh1 = ei == eid1
    cnt = jnp.where(oh0 | oh1, 1.0, 0.0).astype(BF16)
    rr = lax.broadcasted_iota(I32, (tm, tm), 0)
    cc = lax.broadcasted_iota(I32, (tm, tm), 1)
    before = jnp.where(rr < cc, 1.0, 0.0).astype(BF16)
    carry = carry_ref[...]
    tot = _dot(cnt, before) + carry[:, 0:1]
    rank0 = jnp.sum(jnp.where(oh0, tot, 0.0), axis=0, keepdims=True)
    rank1 = jnp.sum(jnp.where(oh1, tot, 0.0), axis=0, keepdims=True)
    new_carry = carry + jnp.sum(cnt.astype(F32), axis=1, keepdims=True)
    carry_ref[...] = new_carry
    cnt_ref[...] = new_carry.astype(I32)
    zi = jnp.zeros((SUBLANES - 4, tm), I32)
    route_ref[...] = jnp.concatenate([eid0, eid1, rank0.astype(I32), rank1.astype(I32), zi], axis=0)
    gpad = jnp.concatenate([g1, g2, jnp.zeros((LANES - 2, tm), F32)], axis=0)
    gate_ref[...] = gpad.T


def _outproj(four, mo, x2d, wof, wom, ln_g, ln_b, wr, br):
    t = x2d.shape[0]
    tm = TM_OUT
    row = lambda w: pl.BlockSpec((tm, w), lambda i: (i, 0))
    return pl.pallas_call(
        _outproj_kernel,
        grid=(t // tm,),
        in_specs=[row(F_W), row(M_W), row(D_MODEL), _resident(wof.shape), _resident(wom.shape),
                  _resident((1, D_MODEL)), _resident((1, D_MODEL)), _resident(wr.shape),
                  _resident((1, ROUTE_COLS))],
        out_specs=[row(D_MODEL), pl.BlockSpec((SUBLANES, tm), lambda i: (0, i)), row(LANES),
                   pl.BlockSpec((N_EXPERTS, LANES), lambda i: (0, 0))],
        out_shape=[jax.ShapeDtypeStruct((t, D_MODEL), F32), jax.ShapeDtypeStruct((SUBLANES, t), I32),
                   jax.ShapeDtypeStruct((t, LANES), F32), jax.ShapeDtypeStruct((N_EXPERTS, LANES), I32)],
        scratch_shapes=[pltpu.VMEM((N_EXPERTS, LANES), F32)],
        compiler_params=_cparams(("arbitrary",), 56),
        name="outproj_router",
    )(four, mo, x2d, wof, wom, ln_g, ln_b, wr, br)


def _dest_kernel(route_ref, pstart_ref, dest_ref):
    tm = route_ref.shape[1]
    r = route_ref[...]
    ei = lax.broadcasted_iota(I32, (N_EXPERTS, tm), 0)
    ps = pstart_ref[:, 0:1]
    d0 = jnp.sum(jnp.where(ei == r[0:1], ps, 0), axis=0, keepdims=True) + r[2:3]
    d1 = jnp.sum(jnp.where(ei == r[1:2], ps, 0), axis=0, keepdims=True) + r[3:4]
    dest_ref[...] = jnp.concatenate([d0, d1, jnp.zeros((SUBLANES - 2, tm), I32)], axis=0)


def _dest(route, pstart_b):
    t = route.shape[1]
    tm = 2048
    return pl.pallas_call(
        _dest_kernel,
        grid=(t // tm,),
        in_specs=[pl.BlockSpec((SUBLANES, tm), lambda i: (0, i)),
                  pl.BlockSpec((N_EXPERTS, LANES), lambda i: (0, 0))],
        out_specs=pl.BlockSpec((SUBLANES, tm), lambda i: (0, i)),
        out_shape=jax.ShapeDtypeStruct((SUBLANES, t), I32),
        compiler_params=_cparams(("parallel",), 32),
        name="moe_dest",
    )(route, pstart_b)


def _row_copy(src, dst, sem):
    return pltpu.make_async_copy(src, dst, sem)


def _pack_rows(x):
    lo = lax.bitcast_convert_type(x[:, :PACK_W].astype(BF16).astype(F32), U32)
    hi = lax.bitcast_convert_type(x[:, PACK_W:].astype(BF16).astype(F32), U32)
    return hi | (lo >> 16)


def _unpack_rows(w):
    lo = lax.bitcast_convert_type(w << 16, F32)
    hi = lax.bitcast_convert_type(w & jnp.uint32(0xFFFF0000), F32)
    return jnp.concatenate([lo, hi], axis=1)


def _rows_to_slabs(words, slab_ref):
    rows = words.shape[0]
    for j in range(ROW_SLABS):
        slab_ref[pl.ds(j, rows, stride=ROW_SLABS), :] = words[:, j * LANES:(j + 1) * LANES]


def _slabs_to_rows(slab_ref):
    rows = slab_ref.shape[0] // ROW_SLABS
    return jnp.concatenate([slab_ref[pl.ds(j, rows, stride=ROW_SLABS), :] for j in range(ROW_SLABS)], axis=1)


def _slab(ref, r):
    if isinstance(r, int):
        return ref.at[pl.ds(r * ROW_SLABS, ROW_SLABS), :]
    return ref.at[pl.ds(pl.multiple_of(r * ROW_SLABS, ROW_SLABS), ROW_SLABS), :]


def _dispatch_kernel(d0_ref, d1_ref, cnt_ref, ps_ref, nu_ref, h_ref, xs_hbm, rows_ref, zero_ref, sem):
    td = h_ref.shape[0]
    zr = zero_ref.shape[0] // ROW_SLABS
    n_blocks = xs_hbm.shape[0] // (MOE_BLK * ROW_SLABS)
    base = pl.program_id(0) * td
    _rows_to_slabs(_pack_rows(h_ref[...]), rows_ref)

    def issue(it, c):
        for u in range(DMA_UNROLL):
            r = it * DMA_UNROLL + u
            _row_copy(_slab(rows_ref, r), _slab(xs_hbm, d0_ref[base + r]), sem).start()
            _row_copy(_slab(rows_ref, r), _slab(xs_hbm, d1_ref[base + r]), sem).start()
        return c

    lax.fori_loop(0, td // DMA_UNROLL, issue, 0)

    def drain(it, c):
        for _ in range(2 * DMA_UNROLL):
            _row_copy(_slab(rows_ref, 0), _slab(xs_hbm, 0), sem).wait()
        return c

    lax.fori_loop(0, td // DMA_UNROLL, drain, 0)

    @pl.when(pl.program_id(0) == pl.num_programs(0) - 1)
    def _():
        zero_ref[...] = jnp.zeros_like(zero_ref)

        def per_expert(e, c):
            n = cnt_ref[e]
            npad = (n + MOE_BLK - 1) // MOE_BLK * MOE_BLK - n
            first = ps_ref[e] + n

            def pad_issue(r, c2):
                _row_copy(_slab(zero_ref, 0), _slab(xs_hbm, first + r), sem).start()
                return c2

            lax.fori_loop(0, npad, pad_issue, 0)

            def pad_drain(r, c2):
                _row_copy(_slab(zero_ref, 0), _slab(xs_hbm, 0), sem).wait()
                return c2

            lax.fori_loop(0, npad, pad_drain, 0)
            return c

        lax.fori_loop(0, N_EXPERTS, per_expert, 0)

        def tail_issue(j, c):
            row0 = pl.multiple_of((nu_ref[0] * MOE_BLK + j * zr) * ROW_SLABS, zr * ROW_SLABS)
            _row_copy(zero_ref, xs_hbm.at[pl.ds(row0, zr * ROW_SLABS), :], sem).start()
            return c

        n_tail = (n_blocks - nu_ref[0]) * (MOE_BLK // zr)
        lax.fori_loop(0, n_tail, tail_issue, 0)

        def tail_drain(j, c):
            _row_copy(zero_ref, xs_hbm.at[pl.ds(0, zr * ROW_SLABS), :], sem).wait()
            return c

        lax.fori_loop(0, n_tail, tail_drain, 0)


def _dispatch(d0, d1, counts, pstart, n_used, h, n_slots):
    t = h.shape[0]
    return pl.pallas_call(
        _dispatch_kernel,
        grid_spec=pltpu.PrefetchScalarGridSpec(
            num_scalar_prefetch=5, grid=(t // TD,),
            in_specs=[pl.BlockSpec((TD, D_MODEL), lambda i, *_: (i, 0))],
            out_specs=pl.BlockSpec(memory_space=pl.ANY),
            scratch_shapes=[pltpu.VMEM((TD * ROW_SLABS, LANES), U32), pltpu.VMEM((ZERO_ROWS * ROW_SLABS, LANES), U32),
                            pltpu.SemaphoreType.DMA]),
        out_shape=jax.ShapeDtypeStruct((n_slots * ROW_SLABS, LANES), U32),
        compiler_params=_cparams(("arbitrary",), 32),
        name="moe_dispatch",
    )(d0, d1, counts, pstart, n_used, h)


def _expert_kernel(be_ref, nu_ref, xs_ref, wg_ref, wu_ref, wd_ref, ys_ref):
    i = pl.program_id(0)

    @pl.when(i < nu_ref[0])
    def _():
        xb = _unpack_rows(_slabs_to_rows(xs_ref)).astype(BF16)
        a = _dot(xb, wg_ref[0, 0])
        u = _dot(xb, wu_ref[0, 0])
        hid = (a * jax.nn.sigmoid(a) * u).astype(BF16)
        _rows_to_slabs(_pack_rows(_dot(hid, wd_ref[0, 0])), ys_ref)

    @pl.when(i >= nu_ref[0])
    def _():
        ys_ref[...] = jnp.zeros_like(ys_ref)


def _experts(blk_e, n_used, xs, wg, wu, wd, layer):
    n_slots = xs.shape[0] // ROW_SLABS
    de = wg.shape[3]
    rows = pl.BlockSpec((MOE_BLK * ROW_SLABS, LANES), lambda i, be, nu: (jnp.minimum(i, nu[0] - 1), 0))
    return pl.pallas_call(
        _expert_kernel,
        grid_spec=pltpu.PrefetchScalarGridSpec(
            num_scalar_prefetch=2, grid=(n_slots // MOE_BLK,),
            in_specs=[rows,
                      pl.BlockSpec((1, 1, D_MODEL, de), lambda i, be, nu: (layer, be[i], 0, 0)),
                      pl.BlockSpec((1, 1, D_MODEL, de), lambda i, be, nu: (layer, be[i], 0, 0)),
                      pl.BlockSpec((1, 1, de, D_MODEL), lambda i, be, nu: (layer, be[i], 0, 0))],
            out_specs=pl.BlockSpec((MOE_BLK * ROW_SLABS, LANES), lambda i, be, nu: (i, 0))),
        out_shape=jax.ShapeDtypeStruct((n_slots * ROW_SLABS, LANES), U32),
        compiler_params=_cparams(("arbitrary",), 56),
        name="moe_experts",
    )(blk_e, n_used, xs, wg, wu, wd)


def _combine_kernel(d0_ref, d1_ref, h_ref, gate_ref, ys_hbm, g_ref, b_ref, o_ref, y0_ref, y1_ref, sem):
    td = h_ref.shape[0]
    step = pl.program_id(0)
    slot = step % 2

    def issue(tile, buf):
        base = tile * td

        def body(it, c):
            for u in range(DMA_UNROLL):
                r = it * DMA_UNROLL + u
                _row_copy(_slab(ys_hbm, d0_ref[base + r]), _slab(y0_ref.at[buf], r), sem.at[buf]).start()
                _row_copy(_slab(ys_hbm, d1_ref[base + r]), _slab(y1_ref.at[buf], r), sem.at[buf]).start()
            return c

        lax.fori_loop(0, td // DMA_UNROLL, body, 0)

    @pl.when(step == 0)
    def _():
        issue(0, 0)

    @pl.when(step + 1 < pl.num_programs(0))
    def _():
        issue(step + 1, 1 - slot)

    def drain(it, c):
        for _ in range(2 * DMA_UNROLL):
            _row_copy(_slab(ys_hbm, 0), _slab(y0_ref.at[slot], 0), sem.at[slot]).wait()
        return c

    lax.fori_loop(0, td // DMA_UNROLL, drain, 0)
    gate = gate_ref[...]
    y0 = _unpack_rows(_slabs_to_rows(y0_ref.at[slot]))
    y1 = _unpack_rows(_slabs_to_rows(y1_ref.at[slot]))
    ffn = y0 * gate[:, 0:1] + y1 * gate[:, 1:2]
    o_ref[...] = _layer_norm_rows(ALPHA * h_ref[...] + ffn, g_ref[...], b_ref[...])


def _combine(d0, d1, h, gate, ys, ln_g, ln_b):
    t = h.shape[0]
    return pl.pallas_call(
        _combine_kernel,
        grid_spec=pltpu.PrefetchScalarGridSpec(
            num_scalar_prefetch=2, grid=(t // TD,),
            in_specs=[pl.BlockSpec((TD, D_MODEL), lambda i, *_: (i, 0)),
                      pl.BlockSpec((TD, LANES), lambda i, *_: (i, 0)),
                      pl.BlockSpec(memory_space=pl.ANY),
                      pl.BlockSpec((1, D_MODEL), lambda i, *_: (0, 0)),
                      pl.BlockSpec((1, D_MODEL), lambda i, *_: (0, 0))],
            out_specs=pl.BlockSpec((TD, D_MODEL), lambda i, *_: (i, 0)),
            scratch_shapes=[pltpu.VMEM((2, TD * ROW_SLABS, LANES), U32), pltpu.VMEM((2, TD * ROW_SLABS, LANES), U32),
                            pltpu.SemaphoreType.DMA((2,))]),
        out_shape=jax.ShapeDtypeStruct((t, D_MODEL), F32),
        compiler_params=_cparams(("arbitrary",), 32),
        name="moe_combine",
    )(d0, d1, h, gate, ys, ln_g, ln_b)


def _channel_dft_matrix():
    c = np.arange(F_GW, dtype=np.int64)
    ang = 2.0 * np.pi * ((c[:, None] * c[None, :]) % F_GW) / F_GW
    m = np.concatenate([np.cos(ang), -np.sin(ang)], axis=1) / np.sqrt(F_GW)
    return jnp.asarray(m, dtype=BF16)


def _prep_layer(p, l):
    w_in = p["w_in"][l]
    o1 = F_W
    o3 = o1 + 2 * QK_W
    o4 = o3 + M_W
    o5 = o4 + M_W
    n_gate = 4 * M_HEADS
    wg = jnp.pad(w_in[:, o5:], ((0, 0), (0, GATE_COLS - n_gate)))
    bias = jnp.concatenate([p["b_igate"][l].reshape(-1), p["b_fgate"][l].reshape(-1),
                            jnp.zeros((GATE_COLS - n_gate,), F32)])
    wr = jnp.zeros((D_MODEL, ROUTE_COLS), F32)
    wr = wr.at[:, :N_GROUPS].set(p["w_coarse"][l]).at[:, FINE_OFF:FINE_OFF + N_EXPERTS].set(p["w_fine"][l])
    br = jnp.zeros((ROUTE_COLS,), F32)
    br = br.at[:N_GROUPS].set(p["b_coarse"][l]).at[FINE_OFF:FINE_OFF + N_EXPERTS].set(p["b_fine"][l])
    w_out = p["w_out"][l]
    return dict(
        wf=w_in[:, :o1].astype(BF16), wqk=w_in[:, o1:o3].astype(BF16), wv=w_in[:, o3:o4].astype(BF16),
        wo=w_in[:, o4:o5].astype(BF16), wg=wg.astype(BF16),
        conv=p["conv_qk"][l], bcol=bias[None, :], brow=jnp.broadcast_to(bias[:, None], (GATE_COLS, LANES)),
        hng=p["hn_g"][l][None, :],
        wof=w_out[:F_W].astype(BF16), wom=w_out[F_W:].astype(BF16),
        ln1g=p["ln1_g"][l][None, :], ln1b=p["ln1_b"][l][None, :],
        wr=wr.astype(BF16), br=br[None, :],
        weg=p["weg"], weu=p["weu"], wed=p["wed"], layer=l,
        ln2g=p["ln2_g"][l][None, :], ln2b=p["ln2_b"][l][None, :],
    )


def _moe(h, route, gate, counts_b, w):
    t = h.shape[0]
    n_blocks = (t * 2) // MOE_BLK + N_EXPERTS
    counts = counts_b[:, 0]
    padded = (counts + MOE_BLK - 1) // MOE_BLK * MOE_BLK
    pend = jnp.cumsum(padded)
    pstart = (pend - padded).astype(I32)
    n_used = (pend[-1] // MOE_BLK).astype(I32)
    first_row = jnp.minimum(jnp.arange(n_blocks, dtype=I32), n_used - 1) * MOE_BLK
    blk_e = jnp.sum((pend[None, :] <= first_row[:, None]).astype(I32), axis=1)
    blk_e = jnp.minimum(blk_e, N_EXPERTS - 1)
    dest = _dest(route, jnp.broadcast_to(pstart[:, None], (N_EXPERTS, LANES)))
    d0, d1 = dest[0], dest[1]
    xs = _dispatch(d0, d1, counts.astype(I32), pstart, n_used[None], h, n_blocks * MOE_BLK)
    ys = _experts(blk_e, n_used[None], xs, w["weg"], w["weu"], w["wed"], w["layer"])
    return _combine(d0, d1, h, gate, ys, w["ln2g"], w["ln2b"])


def _encode(x, ln_in_g, ln_in_b, wc, layers):
    b, s, d = x.shape
    t = b * s
    x2d = x.reshape(t, d)
    for l, w in enumerate(layers):
        outs = _inproj(x2d, ln_in_g, ln_in_b, w["wf"], wc, w["wqk"], w["wv"], w["wo"], w["wg"], l == 0)
        if l == 0:
            x2d, outs = outs[0], outs[1:]
        xr, xi, zqk, zv, zo, gcol, grow = outs
        four = _seq_dft(xr.reshape(b, s, F_W), xi.reshape(b, s, F_W)).reshape(t, F_W)
        zqk3 = zqk.reshape(b, s, 2 * QK_W)
        zv3 = zv.reshape(b, s, M_W)
        gcol3 = gcol.reshape(b, s, GATE_COLS)
        hfwd = _mlstm(False, zqk3, w["conv"], zv3, gcol3, grow, w["bcol"], w["brow"])
        mo = _mlstm(True, zqk3, w["conv"], zv3, gcol3, grow, w["bcol"], w["brow"],
                    hfwd, zo.reshape(b, s, M_W), w["hng"])
        h, route, gate, counts_b = _outproj(four, mo.reshape(t, M_W), x2d, w["wof"], w["wom"],
                                            w["ln1g"], w["ln1b"], w["wr"], w["br"])
        x2d = _moe(h, route, gate, counts_b, w)
    return x2d.reshape(b, s, d)


def kernel(x_prompt, x_sample, ln_in_g, ln_in_b, w_in, conv_qk, b_igate, b_fgate, hn_g, w_out, ln1_g, ln1_b,
           w_coarse, b_coarse, w_fine, b_fine, w_e_gate, w_e_up, w_e_down, ln2_g, ln2_b):
    p = dict(w_in=w_in, conv_qk=conv_qk, b_igate=b_igate, b_fgate=b_fgate, hn_g=hn_g, w_out=w_out,
             ln1_g=ln1_g, ln1_b=ln1_b, w_coarse=w_coarse, b_coarse=b_coarse, w_fine=w_fine, b_fine=b_fine,
             weg=w_e_gate.astype(BF16), weu=w_e_up.astype(BF16), wed=w_e_down.astype(BF16),
             ln2_g=ln2_g, ln2_b=ln2_b)
    layers = [_prep_layer(p, l) for l in range(w_in.shape[0])]
    wc = _channel_dft_matrix()
    g_in, b_in = ln_in_g[None, :], ln_in_b[None, :]
    y_prompt = _encode(x_prompt, g_in, b_in, wc, layers)
    y_sample = _encode(x_sample, g_in, b_in, wc, layers)
    return (y_prompt, y_sample)
```

```python
import functools

import numpy as np
import jax
import jax.numpy as jnp
from jax import lax
from jax.experimental import pallas as pl
from jax.experimental.pallas import tpu as pltpu

F32 = jnp.float32
BF16 = jnp.bfloat16
I32 = jnp.int32
U32 = jnp.uint32

D_MODEL = 2048
F_W = D_MODEL // 4
F_GROUPS = 4
F_GW = F_W // F_GROUPS
M_W = D_MODEL - F_W
M_HEADS = 8
M_DV = M_W // M_HEADS
M_DK = M_DV // 2
QK_W = M_HEADS * M_DK
N_GROUPS = 4
EXP_PER_GROUP = 8
N_EXPERTS = N_GROUPS * EXP_PER_GROUP
DEPTH = 2
ALPHA = (2 * DEPTH) ** 0.25
LN_EPS = 1e-5
HN_EPS = 1e-6
LOG2E = 1.4426950408889634

LANES = 128
SUBLANES = 8
MIB = 1024 * 1024

TM_IN = 256
TM_OUT = 512
CHUNK = 512
STRIP_FWD = 256
STRIP_BWD = 128
DENSE_DFT_MAX = 4096
TK_DFT = 512
FS2 = 128
FT = 8
MOE_BLK = 512
TD = 512
ZERO_ROWS = 256
PACK_W = D_MODEL // 2
ROW_SLABS = PACK_W // LANES
DMA_UNROLL = 8
GATE_COLS = LANES
ROUTE_COLS = LANES
FINE_OFF = SUBLANES


def _dot(a, b):
    return jnp.dot(a, b, preferred_element_type=F32)


def _dot_nt(a, b):
    return lax.dot_general(a, b, (((1,), (1,)), ((), ())), preferred_element_type=F32)


def _dot_tn(a, b):
    return lax.dot_general(a, b, (((0,), (0,)), ((), ())), preferred_element_type=F32)


def _cparams(sem, vmem_mib):
    return pltpu.CompilerParams(dimension_semantics=sem, vmem_limit_bytes=vmem_mib * MIB)


def _resident(shape):
    nd = len(shape)
    return pl.BlockSpec(shape, lambda *_: (0,) * nd, pipeline_mode=pl.Buffered(1))


def _layer_norm_rows(x, g, b):
    mu = jnp.mean(x, -1, keepdims=True)
    xc = x - mu
    var = jnp.mean(xc * xc, -1, keepdims=True)
    return xc * lax.rsqrt(var + LN_EPS) * g + b


def _log_sigmoid(x):
    return jnp.minimum(x, 0.0) - jnp.log(1.0 + jnp.exp(-jnp.abs(x)))


def _inproj_kernel(apply_ln, x_ref, g_ref, b_ref, wf_ref, wc_ref, wqk_ref, wv_ref, wo_ref, wg_ref, *outs):
    if apply_ln:
        xn_ref, xr_ref, xi_ref, zqk_ref, zv_ref, zo_ref, gcol_ref, grow_ref = outs
    else:
        xr_ref, xi_ref, zqk_ref, zv_ref, zo_ref, gcol_ref, grow_ref = outs
    x = x_ref[...]
    if apply_ln:
        x = _layer_norm_rows(x, g_ref[...], b_ref[...])
        xn_ref[...] = x
    xb = x.astype(BF16)
    zf = _dot(xb, wf_ref[...]).astype(BF16)
    wc = wc_ref[...]
    for g in range(F_GROUPS):
        c = _dot(zf[:, g * F_GW:(g + 1) * F_GW], wc)
        xr_ref[:, g * F_GW:(g + 1) * F_GW] = c[:, :F_GW]
        xi_ref[:, g * F_GW:(g + 1) * F_GW] = c[:, F_GW:]
    zqk_ref[...] = _dot(xb, wqk_ref[...])
    zv_ref[...] = _dot(xb, wv_ref[...]).astype(BF16)
    zo_ref[...] = _dot(xb, wo_ref[...])
    zg = _dot(xb, wg_ref[...])
    gcol_ref[...] = zg
    grow_ref[...] = zg.T


def _inproj(x2d, ln_g, ln_b, wf, wc, wqk, wv, wo, wg, apply_ln):
    t = x2d.shape[0]
    tm = TM_IN
    row = lambda w: pl.BlockSpec((tm, w), lambda i: (i, 0))
    out_shape = [
        jax.ShapeDtypeStruct((t, F_W), F32), jax.ShapeDtypeStruct((t, F_W), F32),
        jax.ShapeDtypeStruct((t, 2 * QK_W), F32), jax.ShapeDtypeStruct((t, M_W), BF16),
        jax.ShapeDtypeStruct((t, M_W), F32), jax.ShapeDtypeStruct((t, GATE_COLS), F32),
        jax.ShapeDtypeStruct((GATE_COLS, t), F32),
    ]
    out_specs = [row(F_W), row(F_W), row(2 * QK_W), row(M_W), row(M_W), row(GATE_COLS),
                 pl.BlockSpec((GATE_COLS, tm), lambda i: (0, i))]
    if apply_ln:
        out_shape = [jax.ShapeDtypeStruct((t, D_MODEL), F32)] + out_shape
        out_specs = [row(D_MODEL)] + out_specs
    return pl.pallas_call(
        functools.partial(_inproj_kernel, apply_ln),
        grid=(t // tm,),
        in_specs=[row(D_MODEL), _resident((1, D_MODEL)), _resident((1, D_MODEL)),
                  _resident(wf.shape), _resident(wc.shape), _resident(wqk.shape),
                  _resident(wv.shape), _resident(wo.shape), _resident(wg.shape)],
        out_specs=out_specs, out_shape=out_shape,
        compiler_params=_cparams(("parallel",), 56),
        name="inproj",
    )(x2d, ln_g, ln_b, wf, wc, wqk, wv, wo, wg)


def _dft_dense_kernel(fc_ref, fs_ref, xr_ref, xi_ref, o_ref):
    o_ref[0] = (_dot(fc_ref[...], xr_ref[0].astype(BF16))
                + _dot(fs_ref[...], xi_ref[0].astype(BF16)))


def _dft_dense(xr, xi):
    b, s, w = xr.shape
    tk = min(TK_DFT, s)
    k = jnp.arange(s, dtype=I32)
    ang = (2.0 * np.pi / s) * ((k[:, None] * k[None, :]) % s).astype(F32)
    scale = 1.0 / np.sqrt(s)
    fc = (jnp.cos(ang) * scale).astype(BF16)
    fs = (jnp.sin(ang) * scale).astype(BF16)
    return pl.pallas_call(
        _dft_dense_kernel,
        grid=(b, s // tk),
        in_specs=[pl.BlockSpec((tk, s), lambda i, j: (j, 0)), pl.BlockSpec((tk, s), lambda i, j: (j, 0)),
                  pl.BlockSpec((1, s, w), lambda i, j: (i, 0, 0)), pl.BlockSpec((1, s, w), lambda i, j: (i, 0, 0))],
        out_specs=pl.BlockSpec((1, tk, w), lambda i, j: (i, j, 0)),
        out_shape=jax.ShapeDtypeStruct((b, s, w), F32),
        compiler_params=_cparams(("parallel", "arbitrary"), 48),
        name="dft_dense",
    )(fc, fs, xr, xi)


def _dft_stage1_kernel(f_ref, xr_ref, xi_ref, ur_ref, ui_ref):
    s1 = xr_ref.shape[1]
    f = f_ref[...]
    for j in range(FT):
        d = jnp.concatenate([xr_ref[0, :, j, :], xi_ref[0, :, j, :]], axis=0).astype(BF16)
        u = _dot(f, d)
        ur_ref[0, :, j, :] = u[:s1]
        ui_ref[0, :, j, :] = u[s1:]


def _dft_stage2_kernel(t_ref, ur_ref, ui_ref, o_ref):
    for j in range(FT):
        d = jnp.concatenate([ur_ref[0, j], ui_ref[0, j]], axis=0).astype(BF16)
        o_ref[0, :, j, :] = _dot(t_ref[j], d)


def _dft_two_stage(xr, xi):
    b, s, w = xr.shape
    s1 = s // FS2
    assert s1 * FS2 == s and s1 % FT == 0 and FS2 % FT == 0
    xr4 = xr.reshape(b, s1, FS2, w)
    xi4 = xi.reshape(b, s1, FS2, w)
    k1 = np.arange(s1, dtype=np.int64)
    a1 = 2.0 * np.pi * ((k1[:, None] * k1[None, :]) % s1) / s1
    c1, sn1 = np.cos(a1), np.sin(a1)
    f1 = jnp.asarray(np.block([[c1, sn1], [-sn1, c1]]) / np.sqrt(s), dtype=BF16)
    s2 = jnp.arange(FS2, dtype=I32)
    kk = jnp.arange(s1, dtype=I32)[:, None] + s1 * jnp.arange(FS2, dtype=I32)[None, :]
    ang = (2.0 * np.pi / s) * ((kk[:, :, None] * s2[None, None, :]) % s).astype(F32)
    t2 = jnp.concatenate([jnp.cos(ang), jnp.sin(ang)], axis=-1).astype(BF16)
    blk1 = pl.BlockSpec((1, s1, FT, w), lambda i, j: (i, 0, j, 0))
    ur, ui = pl.pallas_call(
        _dft_stage1_kernel,
        grid=(b, FS2 // FT),
        in_specs=[pl.BlockSpec((2 * s1, 2 * s1), lambda i, j: (0, 0)), blk1, blk1],
        out_specs=[blk1, blk1],
        out_shape=[jax.ShapeDtypeStruct((b, s1, FS2, w), F32)] * 2,
        compiler_params=_cparams(("parallel", "parallel"), 48),
        name="dft_stage1",
    )(f1, xr4, xi4)
    blk2 = pl.BlockSpec((1, FT, FS2, w), lambda i, j: (i, j, 0, 0))
    y = pl.pallas_call(
        _dft_stage2_kernel,
        grid=(b, s1 // FT),
        in_specs=[pl.BlockSpec((FT, FS2, 2 * FS2), lambda i, j: (j, 0, 0)), blk2, blk2],
        out_specs=pl.BlockSpec((1, FS2, FT, w), lambda i, j: (i, 0, j, 0)),
        out_shape=jax.ShapeDtypeStruct((b, FS2, s1, w), F32),
        compiler_params=_cparams(("parallel", "parallel"), 48),
        name="dft_stage2",
    )(t2, ur, ui)
    return y.reshape(b, s, w)


def _seq_dft(xr, xi):
    if xr.shape[1] <= DENSE_DFT_MAX:
        return _dft_dense(xr, xi)
    return _dft_two_stage(xr, xi)


def _split3(x):
    hi = x.astype(BF16)
    r = x - hi.astype(F32)
    mid = r.astype(BF16)
    lo = (r - mid.astype(F32)).astype(BF16)
    return hi, mid, lo


def _mlstm_kernel(reverse, nc, zqk_ref, hp_ref, hn_ref, cw_ref, zv_ref, gcol_ref, grow_ref,
                  bcol_ref, brow_ref, *rest):
    if reverse:
        hf_ref, zo_ref, hng_ref, out_ref, c_ref, n_ref, m_ref, acc_ref = rest
    else:
        out_ref, c_ref, n_ref, m_ref = rest
    L = zqk_ref.shape[1]
    step = pl.program_id(1)
    cidx = (nc - 1 - step) if reverse else step

    @pl.when(step == 0)
    def _():
        c_ref[...] = jnp.zeros_like(c_ref)
        n_ref[...] = jnp.zeros_like(n_ref)
        m_ref[...] = jnp.zeros_like(m_ref)

    d = 1 if reverse else 0
    i_lo = M_HEADS * d
    f_lo = 2 * M_HEADS + M_HEADS * d

    z = zqk_ref[0]
    rowi = lax.broadcasted_iota(I32, (L, 1), 0)
    prev = jnp.where(cidx > 0, hp_ref[0][SUBLANES - 1:SUBLANES, :], 0.0)
    nxt = jnp.where(cidx < nc - 1, hn_ref[0][0:1, :], 0.0)
    zm1 = jnp.where(rowi == 0, prev, pltpu.roll(z, 1, 0))
    zp1 = jnp.where(rowi == L - 1, nxt, pltpu.roll(z, L - 1, 0))
    cw = cw_ref[...]
    conv = zm1 * cw[0:1] + z * cw[1:2] + zp1 * cw[2:3]
    qk = conv * jax.nn.sigmoid(conv)
    k_all = qk[:, QK_W:]
    qt_all = (qk[:, :QK_W] * (M_DK ** -0.5)).T
    kt_all = k_all.T

    gcol = gcol_ref[0] + bcol_ref[...]
    grow = grow_ref[...] + brow_ref[:, 0:1]
    lf_col = _log_sigmoid(gcol) * LOG2E
    lf_row = _log_sigmoid(grow[f_lo:f_lo + M_HEADS, :]) * LOG2E
    ig_row = grow[i_lo:i_lo + M_HEADS, :] * LOG2E

    si = lax.broadcasted_iota(I32, (L, L), 0)
    ti = lax.broadcasted_iota(I32, (L, L), 1)
    feeds = (si >= ti) if reverse else (si <= ti)
    feeds_b = jnp.where(feeds, 1.0, 0.0).astype(BF16)
    feeds_tb = jnp.where((ti >= si) if reverse else (ti <= si), 1.0, 0.0).astype(BF16)
    a_col = sum(_dot(feeds_tb, p) for p in _split3(lf_col))
    a_row = sum(_dot(p, feeds_b) for p in _split3(lf_row))
    r_col = pltpu.roll(gcol, 2 * M_HEADS, 1) * LOG2E - a_col
    r_row = ig_row - a_row
    edge = 0 if reverse else L - 1
    g_tot = a_row[:, edge:edge + 1]
    m_st = m_ref[:, 0:1]
    lane = lax.broadcasted_iota(I32, (1, L), 1)
    run = r_row
    k = 1
    while k < L:
        if reverse:
            shifted = jnp.where(lane < L - k, pltpu.roll(run, L - k, 1), -jnp.inf)
        else:
            shifted = jnp.where(lane >= k, pltpu.roll(run, k, 1), -jnp.inf)
        run = jnp.maximum(run, shifted)
        k *= 2
    u_row = jnp.maximum(m_st, run)
    w_inter_all = jnp.exp2(m_st - u_row)
    floor_all = jnp.exp2(-(a_row + u_row))
    wl_row = g_tot - a_row + ig_row
    m_new = jnp.maximum(g_tot + m_st, jnp.max(wl_row, axis=1, keepdims=True))
    decay = jnp.exp2(g_tot + m_st - m_new)
    e_row = jnp.exp2(wl_row - m_new)

    strip = STRIP_BWD if reverse else STRIP_FWD
    sw = strip if (L > strip and L % strip == 0) else L
    strips = []
    for t0 in range(0, L, sw):
        s0, s1 = (t0, L) if reverse else (0, t0 + sw)
        ss = lax.broadcasted_iota(I32, (s1 - s0, sw), 0) + s0
        tt = lax.broadcasted_iota(I32, (s1 - s0, sw), 1) + t0
        strips.append((t0, s0, s1, (ss >= tt) if reverse else (ss <= tt)))

    zv = zv_ref[0]
    for h in range(M_HEADS):
        ch = f_lo + h
        lo, hi = h * M_DV, (h + 1) * M_DV
        kb = k_all[:, h * M_DK:(h + 1) * M_DK].astype(BF16)
        qt = qt_all[h * M_DK:(h + 1) * M_DK, :]
        qtb = qt.astype(BF16)
        kt = kt_all[h * M_DK:(h + 1) * M_DK, :]
        vb = zv[:, lo:hi]
        c_st = c_ref[h]
        n_st = n_ref[h]
        cb = c_st.astype(BF16)
        r_src = r_col[:, ch:ch + 1]
        u_tgt = u_row[h:h + 1, :]
        w_inter = w_inter_all[h:h + 1, :]
        floor = floor_all[h:h + 1, :]
        nq = jnp.sum(qt * n_st, axis=0, keepdims=True)
        for t0, s0, s1, feeds in strips:
            tl = slice(t0, t0 + sw)
            expo = jnp.where(feeds, r_src[s0:s1] - u_tgt[:, tl], -jnp.inf)
            s_t = _dot(kb[s0:s1], qtb[:, tl]) * jnp.exp2(expo)
            den = jnp.sum(s_t, axis=0, keepdims=True) + w_inter[:, tl] * nq[:, tl]
            inv = 1.0 / jnp.maximum(jnp.abs(den), floor[:, tl])
            lhs = jnp.concatenate([(s_t * inv).astype(BF16),
                                   (qt[:, tl] * (w_inter[:, tl] * inv)).astype(BF16)], axis=0)
            rhs = jnp.concatenate([vb[s0:s1], cb], axis=0)
            hout = _dot_tn(lhs, rhs)
            if reverse:
                acc_ref[tl, lo:hi] = hout
            else:
                out_ref[0, tl, lo:hi] = hout
        kw_t = kt * e_row[h:h + 1, :]
        dec = decay[h:h + 1, :]
        c_ref[h] = dec * c_st + _dot(kw_t.astype(BF16), vb)
        n_ref[h] = dec * n_st + jnp.sum(kw_t, axis=1, keepdims=True)
    m_ref[...] = jnp.broadcast_to(m_new, m_ref.shape)
    if reverse:
        pw = 2 * M_DV
        first = lax.broadcasted_iota(I32, (1, pw), 1) < M_DV
        for p in range(M_HEADS // 2):
            lo, hi = p * pw, (p + 1) * pw
            hs = acc_ref[:, lo:hi] + hf_ref[0, :, lo:hi]
            s_a = jnp.sum(jnp.where(first, hs, 0.0), -1, keepdims=True)
            s_b = jnp.sum(jnp.where(first, 0.0, hs), -1, keepdims=True)
            hc = hs - jnp.where(first, s_a, s_b) * (1.0 / M_DV)
            sq = hc * hc
            v_a = jnp.sum(jnp.where(first, sq, 0.0), -1, keepdims=True)
            v_b = jnp.sum(jnp.where(first, 0.0, sq), -1, keepdims=True)
            rstd = jnp.where(first, lax.rsqrt(v_a * (1.0 / M_DV) + HN_EPS), lax.rsqrt(v_b * (1.0 / M_DV) + HN_EPS))
            y = hc * rstd * hng_ref[:, lo:hi] * jax.nn.sigmoid(zo_ref[0, :, lo:hi])
            out_ref[0, :, lo:hi] = y.astype(BF16)


def _mlstm(reverse, zqk, conv_w, zv, gcol, grow, bcol, brow, hfwd=None, zo=None, hng=None):
    b, s, _ = zqk.shape
    L = min(CHUNK, s)
    nc = s // L
    hb = L // SUBLANES
    nhb = s // SUBLANES
    pos = (lambda c: nc - 1 - c) if reverse else (lambda c: c)
    chunk = lambda w: pl.BlockSpec((1, L, w), lambda i, c: (i, pos(c), 0))
    in_specs = [
        chunk(2 * QK_W),
        pl.BlockSpec((1, SUBLANES, 2 * QK_W), lambda i, c: (i, jnp.maximum(pos(c) * hb - 1, 0), 0)),
        pl.BlockSpec((1, SUBLANES, 2 * QK_W), lambda i, c: (i, jnp.minimum((pos(c) + 1) * hb, nhb - 1), 0)),
        pl.BlockSpec((3, 2 * QK_W), lambda i, c: (0, 0)),
        chunk(M_W),
        chunk(GATE_COLS),
        pl.BlockSpec((GATE_COLS, L), lambda i, c: (0, i * nc + pos(c))),
        pl.BlockSpec((1, GATE_COLS), lambda i, c: (0, 0)),
        pl.BlockSpec((GATE_COLS, LANES), lambda i, c: (0, 0)),
    ]
    args = [zqk, zqk, zqk, conv_w, zv, gcol, grow, bcol, brow]
    scratch = [pltpu.VMEM((M_HEADS, M_DK, M_DV), F32), pltpu.VMEM((M_HEADS, M_DK, 1), F32),
               pltpu.VMEM((M_HEADS, LANES), F32)]
    if reverse:
        in_specs += [chunk(M_W), chunk(M_W), pl.BlockSpec((1, M_W), lambda i, c: (0, 0))]
        args += [hfwd, zo, hng]
        scratch += [pltpu.VMEM((L, M_W), F32)]
        out_dtype = BF16
    else:
        out_dtype = F32
    return pl.pallas_call(
        functools.partial(_mlstm_kernel, reverse, nc),
        grid=(b, nc),
        in_specs=in_specs,
        out_specs=chunk(M_W),
        out_shape=jax.ShapeDtypeStruct((b, s, M_W), out_dtype),
        scratch_shapes=scratch,
        compiler_params=_cparams(("parallel", "arbitrary"), 48),
        name="mlstm_bwd" if reverse else "mlstm_fwd",
    )(*args)


def _outproj_kernel(four_ref, mo_ref, x_ref, wof_ref, wom_ref, g_ref, b_ref, wr_ref, br_ref,
                    h_ref, route_ref, gate_ref, cnt_ref, carry_ref):
    tm = x_ref.shape[0]

    @pl.when(pl.program_id(0) == 0)
    def _():
        carry_ref[...] = jnp.zeros_like(carry_ref)

    mix = _dot(four_ref[...].astype(BF16), wof_ref[...]) + _dot(mo_ref[...], wom_ref[...])
    h = _layer_norm_rows(ALPHA * x_ref[...] + mix, g_ref[...], b_ref[...])
    h_ref[...] = h
    logits = _dot(h.astype(BF16), wr_ref[...]) + br_ref[...]
    lt = logits.T
    ri = lax.broadcasted_iota(I32, (SUBLANES, tm), 0)
    lc = jnp.where(ri < N_GROUPS, lt[0:SUBLANES], -jnp.inf)
    cmax = jnp.max(lc, axis=0, keepdims=True)
    grp = jnp.min(jnp.where(lc == cmax, ri, SUBLANES), axis=0, keepdims=True)
    p_grp = 1.0 / jnp.sum(jnp.exp(lc - cmax), axis=0, keepdims=True)
    sel = jnp.zeros((EXP_PER_GROUP, tm), F32)
    for g in range(N_GROUPS):
        lo = FINE_OFF + g * EXP_PER_GROUP
        sel = jnp.where(grp == g, lt[lo:lo + EXP_PER_GROUP], sel)
    v1 = jnp.max(sel, axis=0, keepdims=True)
    j1 = jnp.min(jnp.where(sel == v1, ri, SUBLANES), axis=0, keepdims=True)
    rest = jnp.where(ri == j1, -jnp.inf, sel)
    v2 = jnp.max(rest, axis=0, keepdims=True)
    j2 = jnp.min(jnp.where(rest == v2, ri, SUBLANES), axis=0, keepdims=True)
    e21 = jnp.exp(v2 - v1)
    g1 = p_grp / (1.0 + e21)
    g2 = p_grp * e21 / (1.0 + e21)
    eid0 = grp * EXP_PER_GROUP + j1
    eid1 = grp * EXP_PER_GROUP + j2

    ei = lax.broadcasted_iota(I32, (N_EXPERTS, tm), 0)
    oh0 = ei == eid0
    oh1 = ei == eid1
    cnt = jnp.where(oh0 | oh1, 1.0, 0.0).astype(BF16)
    rr = lax.broadcasted_iota(I32, (tm, tm), 0)
    cc = lax.broadcasted_iota(I32, (tm, tm), 1)
    before = jnp.where(rr < cc, 1.0, 0.0).astype(BF16)
    carry = carry_ref[...]
    tot = _dot(cnt, before) + carry[:, 0:1]
    rank0 = jnp.sum(jnp.where(oh0, tot, 0.0), axis=0, keepdims=True)
    rank1 = jnp.sum(jnp.where(oh1, tot, 0.0), axis=0, keepdims=True)
    new_carry = carry + jnp.sum(cnt.astype(F32), axis=1, keepdims=True)
    carry_ref[...] = new_carry
    cnt_ref[...] = new_carry.astype(I32)
    zi = jnp.zeros((SUBLANES - 4, tm), I32)
    route_ref[...] = jnp.concatenate([eid0, eid1, rank0.astype(I32), rank1.astype(I32), zi], axis=0)
    gpad = jnp.concatenate([g1, g2, jnp.zeros((LANES - 2, tm), F32)], axis=0)
    gate_ref[...] = gpad.T


def _outproj(four, mo, x2d, wof, wom, ln_g, ln_b, wr, br):
    t = x2d.shape[0]
    tm = TM_OUT
    row = lambda w: pl.BlockSpec((tm, w), lambda i: (i, 0))
    return pl.pallas_call(
        _outproj_kernel,
        grid=(t // tm,),
        in_specs=[row(F_W), row(M_W), row(D_MODEL), _resident(wof.shape), _resident(wom.shape),
                  _resident((1, D_MODEL)), _resident((1, D_MODEL)), _resident(wr.shape),
                  _resident((1, ROUTE_COLS))],
        out_specs=[row(D_MODEL), pl.BlockSpec((SUBLANES, tm), lambda i: (0, i)), row(LANES),
                   pl.BlockSpec((N_EXPERTS, LANES), lambda i: (0, 0))],
        out_shape=[jax.ShapeDtypeStruct((t, D_MODEL), F32), jax.ShapeDtypeStruct((SUBLANES, t), I32),
                   jax.ShapeDtypeStruct((t, LANES), F32), jax.ShapeDtypeStruct((N_EXPERTS, LANES), I32)],
        scratch_shapes=[pltpu.VMEM((N_EXPERTS, LANES), F32)],
        compiler_params=_cparams(("arbitrary",), 56),
        name="outproj_router",
    )(four, mo, x2d, wof, wom, ln_g, ln_b, wr, br)


def _dest_kernel(route_ref, pstart_ref, dest_ref):
    tm = route_ref.shape[1]
    r = route_ref[...]
    ei = lax.broadcasted_iota(I32, (N_EXPERTS, tm), 0)
    ps = pstart_ref[:, 0:1]
    d0 = jnp.sum(jnp.where(ei == r[0:1], ps, 0), axis=0, keepdims=True) + r[2:3]
    d1 = jnp.sum(jnp.where(ei == r[1:2], ps, 0), axis=0, keepdims=True) + r[3:4]
    dest_ref[...] = jnp.concatenate([d0, d1, jnp.zeros((SUBLANES - 2, tm), I32)], axis=0)


def _dest(route, pstart_b):
    t = route.shape[1]
    tm = 2048
    return pl.pallas_call(
        _dest_kernel,
        grid=(t // tm,),
        in_specs=[pl.BlockSpec((SUBLANES, tm), lambda i: (0, i)),
                  pl.BlockSpec((N_EXPERTS, LANES), lambda i: (0, 0))],
        out_specs=pl.BlockSpec((SUBLANES, tm), lambda i: (0, i)),
        out_shape=jax.ShapeDtypeStruct((SUBLANES, t), I32),
        compiler_params=_cparams(("parallel",), 32),
        name="moe_dest",
    )(route, pstart_b)


def _row_copy(src, dst, sem):
    return pltpu.make_async_copy(src, dst, sem)


def _pack_rows(x):
    lo = lax.bitcast_convert_type(x[:, :PACK_W].astype(BF16).astype(F32), U32)
    hi = lax.bitcast_convert_type(x[:, PACK_W:].astype(BF16).astype(F32), U32)
    return hi | (lo >> 16)


def _unpack_rows(w):
    lo = lax.bitcast_convert_type(w << 16, F32)
    hi = lax.bitcast_convert_type(w & jnp.uint32(0xFFFF0000), F32)
    return jnp.concatenate([lo, hi], axis=1)


def _rows_to_slabs(words, slab_ref):
    rows = words.shape[0]
    for j in range(ROW_SLABS):
        slab_ref[pl.ds(j, rows, stride=ROW_SLABS), :] = words[:, j * LANES:(j + 1) * LANES]


def _slabs_to_rows(slab_ref):
    rows = slab_ref.shape[0] // ROW_SLABS
    return jnp.concatenate([slab_ref[pl.ds(j, rows, stride=ROW_SLABS), :] for j in range(ROW_SLABS)], axis=1)


def _slab(ref, r):
    if isinstance(r, int):
        return ref.at[pl.ds(r * ROW_SLABS, ROW_SLABS), :]
    return ref.at[pl.ds(pl.multiple_of(r * ROW_SLABS, ROW_SLABS), ROW_SLABS), :]


def _dispatch_kernel(d0_ref, d1_ref, cnt_ref, ps_ref, nu_ref, h_ref, xs_hbm, rows_ref, zero_ref, sem):
    td = h_ref.shape[0]
    zr = zero_ref.shape[0] // ROW_SLABS
    n_blocks = xs_hbm.shape[0] // (MOE_BLK * ROW_SLABS)
    base = pl.program_id(0) * td
    _rows_to_slabs(_pack_rows(h_ref[...]), rows_ref)

    def issue(it, c):
        for u in range(DMA_UNROLL):
            r = it * DMA_UNROLL + u
            _row_copy(_slab(rows_ref, r), _slab(xs_hbm, d0_ref[base + r]), sem).start(priority=0)
            _row_copy(_slab(rows_ref, r), _slab(xs_hbm, d1_ref[base + r]), sem).start(priority=1)
        return c

    lax.fori_loop(0, td // DMA_UNROLL, issue, 0)

    def drain(it, c):
        for _ in range(2 * DMA_UNROLL):
            _row_copy(_slab(rows_ref, 0), _slab(xs_hbm, 0), sem).wait()
        return c

    lax.fori_loop(0, td // DMA_UNROLL, drain, 0)

    @pl.when(pl.program_id(0) == pl.num_programs(0) - 1)
    def _():
        zero_ref[...] = jnp.zeros_like(zero_ref)

        def per_expert(e, c):
            n = cnt_ref[e]
            npad = (n + MOE_BLK - 1) // MOE_BLK * MOE_BLK - n
            first = ps_ref[e] + n

            def pad_issue(r, c2):
                _row_copy(_slab(zero_ref, 0), _slab(xs_hbm, first + r), sem).start()
                return c2

            lax.fori_loop(0, npad, pad_issue, 0)

            def pad_drain(r, c2):
                _row_copy(_slab(zero_ref, 0), _slab(xs_hbm, 0), sem).wait()
                return c2

            lax.fori_loop(0, npad, pad_drain, 0)
            return c

        lax.fori_loop(0, N_EXPERTS, per_expert, 0)

        def tail_issue(j, c):
            row0 = pl.multiple_of((nu_ref[0] * MOE_BLK + j * zr) * ROW_SLABS, zr * ROW_SLABS)
            _row_copy(zero_ref, xs_hbm.at[pl.ds(row0, zr * ROW_SLABS), :], sem).start()
            return c

        n_tail = (n_blocks - nu_ref[0]) * (MOE_BLK // zr)
        lax.fori_loop(0, n_tail, tail_issue, 0)

        def tail_drain(j, c):
            _row_copy(zero_ref, xs_hbm.at[pl.ds(0, zr * ROW_SLABS), :], sem).wait()
            return c

        lax.fori_loop(0, n_tail, tail_drain, 0)


def _dispatch(d0, d1, counts, pstart, n_used, h, n_slots):
    t = h.shape[0]
    return pl.pallas_call(
        _dispatch_kernel,
        grid_spec=pltpu.PrefetchScalarGridSpec(
            num_scalar_prefetch=5, grid=(t // TD,),
            in_specs=[pl.BlockSpec((TD, D_MODEL), lambda i, *_: (i, 0))],
            out_specs=pl.BlockSpec(memory_space=pl.ANY),
            scratch_shapes=[pltpu.VMEM((TD * ROW_SLABS, LANES), U32), pltpu.VMEM((ZERO_ROWS * ROW_SLABS, LANES), U32),
                            pltpu.SemaphoreType.DMA]),
        out_shape=jax.ShapeDtypeStruct((n_slots * ROW_SLABS, LANES), U32),
        compiler_params=_cparams(("arbitrary",), 32),
        name="moe_dispatch",
    )(d0, d1, counts, pstart, n_used, h)


def _expert_kernel(be_ref, nu_ref, xs_ref, wg_ref, wu_ref, wd_ref, ys_ref):
    i = pl.program_id(0)

    @pl.when(i < nu_ref[0])
    def _():
        xb = _unpack_rows(_slabs_to_rows(xs_ref)).astype(BF16)
        a = _dot(xb, wg_ref[0, 0])
        u = _dot(xb, wu_ref[0, 0])
        hid = (a * jax.nn.sigmoid(a) * u).astype(BF16)
        _rows_to_slabs(_pack_rows(_dot(hid, wd_ref[0, 0])), ys_ref)

    @pl.when(i >= nu_ref[0])
    def _():
        ys_ref[...] = jnp.zeros_like(ys_ref)


def _experts(blk_e, n_used, xs, wg, wu, wd, layer):
    n_slots = xs.shape[0] // ROW_SLABS
    de = wg.shape[3]
    rows = pl.BlockSpec((MOE_BLK * ROW_SLABS, LANES), lambda i, be, nu: (jnp.minimum(i, nu[0] - 1), 0))
    return pl.pallas_call(
        _expert_kernel,
        grid_spec=pltpu.PrefetchScalarGridSpec(
            num_scalar_prefetch=2, grid=(n_slots // MOE_BLK,),
            in_specs=[rows,
                      pl.BlockSpec((1, 1, D_MODEL, de), lambda i, be, nu: (layer, be[i], 0, 0)),
                      pl.BlockSpec((1, 1, D_MODEL, de), lambda i, be, nu: (layer, be[i], 0, 0)),
                      pl.BlockSpec((1, 1, de, D_MODEL), lambda i, be, nu: (layer, be[i], 0, 0))],
            out_specs=pl.BlockSpec((MOE_BLK * ROW_SLABS, LANES), lambda i, be, nu: (i, 0))),
        out_shape=jax.ShapeDtypeStruct((n_slots * ROW_SLABS, LANES), U32),
        compiler_params=_cparams(("arbitrary",), 56),
        name="moe_experts",
    )(blk_e, n_used, xs, wg, wu, wd)


def _combine_kernel(d0_ref, d1_ref, h_ref, gate_ref, ys_hbm, g_ref, b_ref, o_ref, y0_ref, y1_ref, sem):
    td = h_ref.shape[0]
    step = pl.program_id(0)
    slot = step % 2

    def issue(tile, buf):
        base = tile * td

        def body(it, c):
            for u in range(DMA_UNROLL):
                r = it * DMA_UNROLL + u
                _row_copy(_slab(ys_hbm, d0_ref[base + r]), _slab(y0_ref.at[buf], r), sem.at[buf]).start()
                _row_copy(_slab(ys_hbm, d1_ref[base + r]), _slab(y1_ref.at[buf], r), sem.at[buf]).start()
            return c

        lax.fori_loop(0, td // DMA_UNROLL, body, 0)

    @pl.when(step == 0)
    def _():
        issue(0, 0)

    @pl.when(step + 1 < pl.num_programs(0))
    def _():
        issue(step + 1, 1 - slot)

    def drain(it, c):
        for _ in range(2 * DMA_UNROLL):
            _row_copy(_slab(ys_hbm, 0), _slab(y0_ref.at[slot], 0), sem.at[slot]).wait()
        return c

    lax.fori_loop(0, td // DMA_UNROLL, drain, 0)
    gate = gate_ref[...]
    y0 = _unpack_rows(_slabs_to_rows(y0_ref.at[slot]))
    y1 = _unpack_rows(_slabs_to_rows(y1_ref.at[slot]))
    ffn = y0 * gate[:, 0:1] + y1 * gate[:, 1:2]
    o_ref[...] = _layer_norm_rows(ALPHA * h_ref[...] + ffn, g_ref[...], b_ref[...])


def _combine(d0, d1, h, gate, ys, ln_g, ln_b):
    t = h.shape[0]
    return pl.pallas_call(
        _combine_kernel,
        grid_spec=pltpu.PrefetchScalarGridSpec(
            num_scalar_prefetch=2, grid=(t // TD,),
            in_specs=[pl.BlockSpec((TD, D_MODEL), lambda i, *_: (i, 0)),
                      pl.BlockSpec((TD, LANES), lambda i, *_: (i, 0)),
                      pl.BlockSpec(memory_space=pl.ANY),
                      pl.BlockSpec((1, D_MODEL), lambda i, *_: (0, 0)),
                      pl.BlockSpec((1, D_MODEL), lambda i, *_: (0, 0))],
            out_specs=pl.BlockSpec((TD, D_MODEL), lambda i, *_: (i, 0)),
            scratch_shapes=[pltpu.VMEM((2, TD * ROW_SLABS, LANES), U32), pltpu.VMEM((2, TD * ROW_SLABS, LANES), U32),
                            pltpu.SemaphoreType.DMA((2,))]),
        out_shape=jax.ShapeDtypeStruct((t, D_MODEL), F32),
        compiler_params=_cparams(("arbitrary",), 48),
        name="moe_combine",
    )(d0, d1, h, gate, ys, ln_g, ln_b)


def _channel_dft_matrix():
    c = np.arange(F_GW, dtype=np.int64)
    ang = 2.0 * np.pi * ((c[:, None] * c[None, :]) % F_GW) / F_GW
    m = np.concatenate([np.cos(ang), -np.sin(ang)], axis=1) / np.sqrt(F_GW)
    return jnp.asarray(m, dtype=BF16)


def _prep_layer(p, l):
    w_in = p["w_in"][l]
    o1 = F_W
    o3 = o1 + 2 * QK_W
    o4 = o3 + M_W
    o5 = o4 + M_W
    n_gate = 4 * M_HEADS
    wg = jnp.pad(w_in[:, o5:], ((0, 0), (0, GATE_COLS - n_gate)))
    bias = jnp.concatenate([p["b_igate"][l].reshape(-1), p["b_fgate"][l].reshape(-1),
                            jnp.zeros((GATE_COLS - n_gate,), F32)])
    wr = jnp.zeros((D_MODEL, ROUTE_COLS), F32)
    wr = wr.at[:, :N_GROUPS].set(p["w_coarse"][l]).at[:, FINE_OFF:FINE_OFF + N_EXPERTS].set(p["w_fine"][l])
    br = jnp.zeros((ROUTE_COLS,), F32)
    br = br.at[:N_GROUPS].set(p["b_coarse"][l]).at[FINE_OFF:FINE_OFF + N_EXPERTS].set(p["b_fine"][l])
    w_out = p["w_out"][l]
    return dict(
        wf=w_in[:, :o1].astype(BF16), wqk=w_in[:, o1:o3].astype(BF16), wv=w_in[:, o3:o4].astype(BF16),
        wo=w_in[:, o4:o5].astype(BF16), wg=wg.astype(BF16),
        conv=p["conv_qk"][l], bcol=bias[None, :], brow=jnp.broadcast_to(bias[:, None], (GATE_COLS, LANES)),
        hng=p["hn_g"][l][None, :],
        wof=w_out[:F_W].astype(BF16), wom=w_out[F_W:].astype(BF16),
        ln1g=p["ln1_g"][l][None, :], ln1b=p["ln1_b"][l][None, :],
        wr=wr.astype(BF16), br=br[None, :],
        weg=p["weg"], weu=p["weu"], wed=p["wed"], layer=l,
        ln2g=p["ln2_g"][l][None, :], ln2b=p["ln2_b"][l][None, :],
    )


def _moe(h, route, gate, counts_b, w):
    t = h.shape[0]
    n_blocks = (t * 2) // MOE_BLK + N_EXPERTS
    counts = counts_b[:, 0]
    padded = (counts + MOE_BLK - 1) // MOE_BLK * MOE_BLK
    pend = jnp.cumsum(padded)
    pstart = (pend - padded).astype(I32)
    n_used = (pend[-1] // MOE_BLK).astype(I32)
    first_row = jnp.minimum(jnp.arange(n_blocks, dtype=I32), n_used - 1) * MOE_BLK
    blk_e = jnp.sum((pend[None, :] <= first_row[:, None]).astype(I32), axis=1)
    blk_e = jnp.minimum(blk_e, N_EXPERTS - 1)
    dest = _dest(route, jnp.broadcast_to(pstart[:, None], (N_EXPERTS, LANES)))
    d0, d1 = dest[0], dest[1]
    xs = _dispatch(d0, d1, counts.astype(I32), pstart, n_used[None], h, n_blocks * MOE_BLK)
    ys = _experts(blk_e, n_used[None], xs, w["weg"], w["weu"], w["wed"], w["layer"])
    return _combine(d0, d1, h, gate, ys, w["ln2g"], w["ln2b"])


def _encode(x, ln_in_g, ln_in_b, wc, layers):
    b, s, d = x.shape
    t = b * s
    x2d = x.reshape(t, d)
    for l, w in enumerate(layers):
        outs = _inproj(x2d, ln_in_g, ln_in_b, w["wf"], wc, w["wqk"], w["wv"], w["wo"], w["wg"], l == 0)
        if l == 0:
            x2d, outs = outs[0], outs[1:]
        xr, xi, zqk, zv, zo, gcol, grow = outs
        four = _seq_dft(xr.reshape(b, s, F_W), xi.reshape(b, s, F_W)).reshape(t, F_W)
        zqk3 = zqk.reshape(b, s, 2 * QK_W)
        zv3 = zv.reshape(b, s, M_W)
        gcol3 = gcol.reshape(b, s, GATE_COLS)
        hfwd = _mlstm(False, zqk3, w["conv"], zv3, gcol3, grow, w["bcol"], w["brow"])
        mo = _mlstm(True, zqk3, w["conv"], zv3, gcol3, grow, w["bcol"], w["brow"],
                    hfwd, zo.reshape(b, s, M_W), w["hng"])
        h, route, gate, counts_b = _outproj(four, mo.reshape(t, M_W), x2d, w["wof"], w["wom"],
                                            w["ln1g"], w["ln1b"], w["wr"], w["br"])
        x2d = _moe(h, route, gate, counts_b, w)
    return x2d.reshape(b, s, d)


def kernel(x_prompt, x_sample, ln_in_g, ln_in_b, w_in, conv_qk, b_igate, b_fgate, hn_g, w_out, ln1_g, ln1_b,
           w_coarse, b_coarse, w_fine, b_fine, w_e_gate, w_e_up, w_e_down, ln2_g, ln2_b):
    p = dict(w_in=w_in, conv_qk=conv_qk, b_igate=b_igate, b_fgate=b_fgate, hn_g=hn_g, w_out=w_out,
             ln1_g=ln1_g, ln1_b=ln1_b, w_coarse=w_coarse, b_coarse=b_coarse, w_fine=w_fine, b_fine=b_fine,
             weg=w_e_gate.astype(BF16), weu=w_e_up.astype(BF16), wed=w_e_down.astype(BF16),
             ln2_g=ln2_g, ln2_b=ln2_b)
    layers = [_prep_layer(p, l) for l in range(w_in.shape[0])]
    wc = _channel_dft_matrix()
    g_in, b_in = ln_in_g[None, :], ln_in_b[None, :]
    y_prompt = _encode(x_prompt, g_in, b_in, wc, layers)
    y_sample = _encode(x_sample, g_in, b_in, wc, layers)
    return (y_prompt, y_sample)
```

```python
import functools

import numpy as np
import jax
import jax.numpy as jnp
from jax import lax
from jax.experimental import pallas as pl
from jax.experimental.pallas import tpu as pltpu

F32 = jnp.float32
BF16 = jnp.bfloat16
I32 = jnp.int32
U32 = jnp.uint32

D_MODEL = 2048
F_W = D_MODEL // 4
F_GROUPS = 4
F_GW = F_W // F_GROUPS
M_W = D_MODEL - F_W
M_HEADS = 8
M_DV = M_W // M_HEADS
M_DK = M_DV // 2
QK_W = M_HEADS * M_DK
N_GROUPS = 4
EXP_PER_GROUP = 8
N_EXPERTS = N_GROUPS * EXP_PER_GROUP
DEPTH = 2
ALPHA = (2 * DEPTH) ** 0.25
LN_EPS = 1e-5
HN_EPS = 1e-6
LOG2E = 1.4426950408889634

LANES = 128
SUBLANES = 8
MIB = 1024 * 1024

TM_IN = 256
TM_OUT = 512
CHUNK = 512
STRIP_FWD = 256
STRIP_BWD = 512
DENSE_DFT_MAX = 4096
TK_DFT = 512
FS2 = 128
FT = 8
MOE_BLK = 512
TD = 512
TC = 256
ZERO_ROWS = 256
PACK_W = D_MODEL // 2
ROW_SLABS = PACK_W // LANES
DMA_UNROLL = 8
GATE_COLS = LANES
ROUTE_COLS = LANES
FINE_OFF = SUBLANES


def _dot(a, b):
    return jnp.dot(a, b, preferred_element_type=F32)


def _dot_nt(a, b):
    return lax.dot_general(a, b, (((1,), (1,)), ((), ())), preferred_element_type=F32)


def _dot_tn(a, b):
    return lax.dot_general(a, b, (((0,), (0,)), ((), ())), preferred_element_type=F32)


def _cparams(sem, vmem_mib):
    return pltpu.CompilerParams(dimension_semantics=sem, vmem_limit_bytes=vmem_mib * MIB)


def _resident(shape):
    nd = len(shape)
    return pl.BlockSpec(shape, lambda *_: (0,) * nd, pipeline_mode=pl.Buffered(1))


def _layer_norm_rows(x, g, b):
    mu = jnp.mean(x, -1, keepdims=True)
    xc = x - mu
    var = jnp.mean(xc * xc, -1, keepdims=True)
    return xc * lax.rsqrt(var + LN_EPS) * g + b


def _log_sigmoid(x):
    return jnp.minimum(x, 0.0) - jnp.log(1.0 + jnp.exp(-jnp.abs(x)))


def _inproj_kernel(apply_ln, x_ref, g_ref, b_ref, wf_ref, wc_ref, wqk_ref, wv_ref, wo_ref, wg_ref, *outs):
    if apply_ln:
        xn_ref, xr_ref, xi_ref, zqk_ref, zv_ref, zo_ref, gcol_ref, grow_ref = outs
    else:
        xr_ref, xi_ref, zqk_ref, zv_ref, zo_ref, gcol_ref, grow_ref = outs
    x = x_ref[...]
    if apply_ln:
        x = _layer_norm_rows(x, g_ref[...], b_ref[...])
        xn_ref[...] = x
    xb = x.astype(BF16)
    zf = _dot(xb, wf_ref[...]).astype(BF16)
    wc = wc_ref[...]
    for g in range(F_GROUPS):
        c = _dot(zf[:, g * F_GW:(g + 1) * F_GW], wc)
        xr_ref[:, g * F_GW:(g + 1) * F_GW] = c[:, :F_GW]
        xi_ref[:, g * F_GW:(g + 1) * F_GW] = c[:, F_GW:]
    zqk_ref[...] = _dot(xb, wqk_ref[...])
    zv_ref[...] = _dot(xb, wv_ref[...]).astype(BF16)
    zo_ref[...] = _dot(xb, wo_ref[...])
    zg = _dot(xb, wg_ref[...])
    gcol_ref[...] = zg
    grow_ref[...] = zg.T


def _inproj(x2d, ln_g, ln_b, wf, wc, wqk, wv, wo, wg, apply_ln):
    t = x2d.shape[0]
    tm = TM_IN
    row = lambda w: pl.BlockSpec((tm, w), lambda i: (i, 0))
    out_shape = [
        jax.ShapeDtypeStruct((t, F_W), F32), jax.ShapeDtypeStruct((t, F_W), F32),
        jax.ShapeDtypeStruct((t, 2 * QK_W), F32), jax.ShapeDtypeStruct((t, M_W), BF16),
        jax.ShapeDtypeStruct((t, M_W), F32), jax.ShapeDtypeStruct((t, GATE_COLS), F32),
        jax.ShapeDtypeStruct((GATE_COLS, t), F32),
    ]
    out_specs = [row(F_W), row(F_W), row(2 * QK_W), row(M_W), row(M_W), row(GATE_COLS),
                 pl.BlockSpec((GATE_COLS, tm), lambda i: (0, i))]
    if apply_ln:
        out_shape = [jax.ShapeDtypeStruct((t, D_MODEL), F32)] + out_shape
        out_specs = [row(D_MODEL)] + out_specs
    return pl.pallas_call(
        functools.partial(_inproj_kernel, apply_ln),
        grid=(t // tm,),
        in_specs=[row(D_MODEL), _resident((1, D_MODEL)), _resident((1, D_MODEL)),
                  _resident(wf.shape), _resident(wc.shape), _resident(wqk.shape),
                  _resident(wv.shape), _resident(wo.shape), _resident(wg.shape)],
        out_specs=out_specs, out_shape=out_shape,
        compiler_params=_cparams(("parallel",), 56),
        name="inproj",
    )(x2d, ln_g, ln_b, wf, wc, wqk, wv, wo, wg)


def _dft_dense_kernel(fc_ref, fs_ref, xr_ref, xi_ref, o_ref):
    o_ref[0] = (_dot(fc_ref[...], xr_ref[0].astype(BF16))
                + _dot(fs_ref[...], xi_ref[0].astype(BF16)))


def _dft_dense(xr, xi):
    b, s, w = xr.shape
    tk = min(TK_DFT, s)
    k = jnp.arange(s, dtype=I32)
    ang = (2.0 * np.pi / s) * ((k[:, None] * k[None, :]) % s).astype(F32)
    scale = 1.0 / np.sqrt(s)
    fc = (jnp.cos(ang) * scale).astype(BF16)
    fs = (jnp.sin(ang) * scale).astype(BF16)
    return pl.pallas_call(
        _dft_dense_kernel,
        grid=(b, s // tk),
        in_specs=[pl.BlockSpec((tk, s), lambda i, j: (j, 0)), pl.BlockSpec((tk, s), lambda i, j: (j, 0)),
                  pl.BlockSpec((1, s, w), lambda i, j: (i, 0, 0)), pl.BlockSpec((1, s, w), lambda i, j: (i, 0, 0))],
        out_specs=pl.BlockSpec((1, tk, w), lambda i, j: (i, j, 0)),
        out_shape=jax.ShapeDtypeStruct((b, s, w), F32),
        compiler_params=_cparams(("parallel", "arbitrary"), 48),
        name="dft_dense",
    )(fc, fs, xr, xi)


def _dft_stage1_kernel(f_ref, xr_ref, xi_ref, ur_ref, ui_ref):
    s1 = xr_ref.shape[1]
    f = f_ref[...]
    for j in range(FT):
        d = jnp.concatenate([xr_ref[0, :, j, :], xi_ref[0, :, j, :]], axis=0).astype(BF16)
        u = _dot(f, d)
        ur_ref[0, :, j, :] = u[:s1]
        ui_ref[0, :, j, :] = u[s1:]


def _dft_stage2_kernel(t_ref, ur_ref, ui_ref, o_ref):
    for j in range(FT):
        d = jnp.concatenate([ur_ref[0, j], ui_ref[0, j]], axis=0).astype(BF16)
        o_ref[0, :, j, :] = _dot(t_ref[j], d)


def _dft_two_stage(xr, xi):
    b, s, w = xr.shape
    s1 = s // FS2
    assert s1 * FS2 == s and s1 % FT == 0 and FS2 % FT == 0
    xr4 = xr.reshape(b, s1, FS2, w)
    xi4 = xi.reshape(b, s1, FS2, w)
    k1 = np.arange(s1, dtype=np.int64)
    a1 = 2.0 * np.pi * ((k1[:, None] * k1[None, :]) % s1) / s1
    c1, sn1 = np.cos(a1), np.sin(a1)
    f1 = jnp.asarray(np.block([[c1, sn1], [-sn1, c1]]) / np.sqrt(s), dtype=BF16)
    s2 = jnp.arange(FS2, dtype=I32)
    kk = jnp.arange(s1, dtype=I32)[:, None] + s1 * jnp.arange(FS2, dtype=I32)[None, :]
    ang = (2.0 * np.pi / s) * ((kk[:, :, None] * s2[None, None, :]) % s).astype(F32)
    t2 = jnp.concatenate([jnp.cos(ang), jnp.sin(ang)], axis=-1).astype(BF16)
    blk1 = pl.BlockSpec((1, s1, FT, w), lambda i, j: (i, 0, j, 0))
    ur, ui = pl.pallas_call(
        _dft_stage1_kernel,
        grid=(b, FS2 // FT),
        in_specs=[pl.BlockSpec((2 * s1, 2 * s1), lambda i, j: (0, 0)), blk1, blk1],
        out_specs=[blk1, blk1],
        out_shape=[jax.ShapeDtypeStruct((b, s1, FS2, w), F32)] * 2,
        compiler_params=_cparams(("parallel", "parallel"), 48),
        name="dft_stage1",
    )(f1, xr4, xi4)
    blk2 = pl.BlockSpec((1, FT, FS2, w), lambda i, j: (i, j, 0, 0))
    y = pl.pallas_call(
        _dft_stage2_kernel,
        grid=(b, s1 // FT),
        in_specs=[pl.BlockSpec((FT, FS2, 2 * FS2), lambda i, j: (j, 0, 0)), blk2, blk2],
        out_specs=pl.BlockSpec((1, FS2, FT, w), lambda i, j: (i, 0, j, 0)),
        out_shape=jax.ShapeDtypeStruct((b, FS2, s1, w), F32),
        compiler_params=_cparams(("parallel", "parallel"), 48),
        name="dft_stage2",
    )(t2, ur, ui)
    return y.reshape(b, s, w)


def _seq_dft(xr, xi):
    if xr.shape[1] <= DENSE_DFT_MAX:
        return _dft_dense(xr, xi)
    return _dft_two_stage(xr, xi)


def _split3(x):
    hi = x.astype(BF16)
    r = x - hi.astype(F32)
    mid = r.astype(BF16)
    lo = (r - mid.astype(F32)).astype(BF16)
    return hi, mid, lo


def _mlstm_kernel(reverse, nc, zqk_ref, hp_ref, hn_ref, cw_ref, zv_ref, gcol_ref, grow_ref,
                  bcol_ref, brow_ref, *rest):
    if reverse:
        hf_ref, zo_ref, hng_ref, out_ref, c_ref, n_ref, m_ref, acc_ref = rest
    else:
        out_ref, c_ref, n_ref, m_ref = rest
    L = zqk_ref.shape[1]
    step = pl.program_id(1)
    cidx = (nc - 1 - step) if reverse else step

    @pl.when(step == 0)
    def _():
        c_ref[...] = jnp.zeros_like(c_ref)
        n_ref[...] = jnp.zeros_like(n_ref)
        m_ref[...] = jnp.zeros_like(m_ref)

    d = 1 if reverse else 0
    i_lo = M_HEADS * d
    f_lo = 2 * M_HEADS + M_HEADS * d

    z = zqk_ref[0]
    rowi = lax.broadcasted_iota(I32, (L, 1), 0)
    prev = jnp.where(cidx > 0, hp_ref[0][SUBLANES - 1:SUBLANES, :], 0.0)
    nxt = jnp.where(cidx < nc - 1, hn_ref[0][0:1, :], 0.0)
    zm1 = jnp.where(rowi == 0, prev, pltpu.roll(z, 1, 0))
    zp1 = jnp.where(rowi == L - 1, nxt, pltpu.roll(z, L - 1, 0))
    cw = cw_ref[...]
    conv = zm1 * cw[0:1] + z * cw[1:2] + zp1 * cw[2:3]
    qk = conv * jax.nn.sigmoid(conv)
    k_all = qk[:, QK_W:]
    qt_all = (qk[:, :QK_W] * (M_DK ** -0.5)).T
    kt_all = k_all.T

    gcol = gcol_ref[0] + bcol_ref[...]
    grow = grow_ref[...] + brow_ref[:, 0:1]
    lf_col = _log_sigmoid(gcol) * LOG2E
    lf_row = _log_sigmoid(grow[f_lo:f_lo + M_HEADS, :]) * LOG2E
    ig_row = grow[i_lo:i_lo + M_HEADS, :] * LOG2E

    si = lax.broadcasted_iota(I32, (L, L), 0)
    ti = lax.broadcasted_iota(I32, (L, L), 1)
    feeds = (si >= ti) if reverse else (si <= ti)
    feeds_b = jnp.where(feeds, 1.0, 0.0).astype(BF16)
    feeds_tb = jnp.where((ti >= si) if reverse else (ti <= si), 1.0, 0.0).astype(BF16)
    a_col = sum(_dot(feeds_tb, p) for p in _split3(lf_col))
    a_row = sum(_dot(p, feeds_b) for p in _split3(lf_row))
    r_col = pltpu.roll(gcol, 2 * M_HEADS, 1) * LOG2E - a_col
    r_row = ig_row - a_row
    edge = 0 if reverse else L - 1
    g_tot = a_row[:, edge:edge + 1]
    m_st = m_ref[:, 0:1]
    lane = lax.broadcasted_iota(I32, (1, L), 1)
    run = r_row
    k = 1
    while k < L:
        if reverse:
            shifted = jnp.where(lane < L - k, pltpu.roll(run, L - k, 1), -jnp.inf)
        else:
            shifted = jnp.where(lane >= k, pltpu.roll(run, k, 1), -jnp.inf)
        run = jnp.maximum(run, shifted)
        k *= 2
    u_row = jnp.maximum(m_st, run)
    w_inter_all = jnp.exp2(m_st - u_row)
    floor_all = jnp.exp2(-(a_row + u_row))
    wl_row = g_tot - a_row + ig_row
    m_new = jnp.maximum(g_tot + m_st, jnp.max(wl_row, axis=1, keepdims=True))
    decay = jnp.exp2(g_tot + m_st - m_new)
    e_row = jnp.exp2(wl_row - m_new)

    strip = STRIP_BWD if reverse else STRIP_FWD
    sw = strip if (L > strip and L % strip == 0) else L
    strips = []
    for t0 in range(0, L, sw):
        s0, s1 = (t0, L) if reverse else (0, t0 + sw)
        ss = lax.broadcasted_iota(I32, (s1 - s0, sw), 0) + s0
        tt = lax.broadcasted_iota(I32, (s1 - s0, sw), 1) + t0
        strips.append((t0, s0, s1, (ss >= tt) if reverse else (ss <= tt)))

    zv = zv_ref[0]
    for h in range(M_HEADS):
        ch = f_lo + h
        lo, hi = h * M_DV, (h + 1) * M_DV
        kb = k_all[:, h * M_DK:(h + 1) * M_DK].astype(BF16)
        qt = qt_all[h * M_DK:(h + 1) * M_DK, :]
        qtb = qt.astype(BF16)
        kt = kt_all[h * M_DK:(h + 1) * M_DK, :]
        vb = zv[:, lo:hi]
        c_st = c_ref[h]
        n_st = n_ref[h]
        cb = c_st.astype(BF16)
        r_src = r_col[:, ch:ch + 1]
        u_tgt = u_row[h:h + 1, :]
        w_inter = w_inter_all[h:h + 1, :]
        floor = floor_all[h:h + 1, :]
        nq = jnp.sum(qt * n_st, axis=0, keepdims=True)
        for t0, s0, s1, feeds in strips:
            tl = slice(t0, t0 + sw)
            expo = jnp.where(feeds, r_src[s0:s1] - u_tgt[:, tl], -jnp.inf)
            s_t = _dot(kb[s0:s1], qtb[:, tl]) * jnp.exp2(expo)
            den = jnp.sum(s_t, axis=0, keepdims=True) + w_inter[:, tl] * nq[:, tl]
            inv = 1.0 / jnp.maximum(jnp.abs(den), floor[:, tl])
            lhs = jnp.concatenate([(s_t * inv).astype(BF16),
                                   (qt[:, tl] * (w_inter[:, tl] * inv)).astype(BF16)], axis=0)
            rhs = jnp.concatenate([vb[s0:s1], cb], axis=0)
            hout = _dot_tn(lhs, rhs)
            if reverse:
                acc_ref[tl, lo:hi] = hout
            else:
                out_ref[0, tl, lo:hi] = hout
        kw_t = kt * e_row[h:h + 1, :]
        dec = decay[h:h + 1, :]
        c_ref[h] = dec * c_st + _dot(kw_t.astype(BF16), vb)
        n_ref[h] = dec * n_st + jnp.sum(kw_t, axis=1, keepdims=True)
    m_ref[...] = jnp.broadcast_to(m_new, m_ref.shape)
    if reverse:
        pw = 2 * M_DV
        first = lax.broadcasted_iota(I32, (1, pw), 1) < M_DV
        for p in range(M_HEADS // 2):
            lo, hi = p * pw, (p + 1) * pw
            hs = acc_ref[:, lo:hi] + hf_ref[0, :, lo:hi]
            s_a = jnp.sum(jnp.where(first, hs, 0.0), -1, keepdims=True)
            s_b = jnp.sum(jnp.where(first, 0.0, hs), -1, keepdims=True)
            hc = hs - jnp.where(first, s_a, s_b) * (1.0 / M_DV)
            sq = hc * hc
            v_a = jnp.sum(jnp.where(first, sq, 0.0), -1, keepdims=True)
            v_b = jnp.sum(jnp.where(first, 0.0, sq), -1, keepdims=True)
            rstd = jnp.where(first, lax.rsqrt(v_a * (1.0 / M_DV) + HN_EPS), lax.rsqrt(v_b * (1.0 / M_DV) + HN_EPS))
            y = hc * rstd * hng_ref[:, lo:hi] * jax.nn.sigmoid(zo_ref[0, :, lo:hi])
            out_ref[0, :, lo:hi] = y.astype(BF16)


def _mlstm(reverse, zqk, conv_w, zv, gcol, grow, bcol, brow, hfwd=None, zo=None, hng=None):
    b, s, _ = zqk.shape
    L = min(CHUNK, s)
    nc = s // L
    hb = L // SUBLANES
    nhb = s // SUBLANES
    pos = (lambda c: nc - 1 - c) if reverse else (lambda c: c)
    chunk = lambda w: pl.BlockSpec((1, L, w), lambda i, c: (i, pos(c), 0))
    in_specs = [
        chunk(2 * QK_W),
        pl.BlockSpec((1, SUBLANES, 2 * QK_W), lambda i, c: (i, jnp.maximum(pos(c) * hb - 1, 0), 0)),
        pl.BlockSpec((1, SUBLANES, 2 * QK_W), lambda i, c: (i, jnp.minimum((pos(c) + 1) * hb, nhb - 1), 0)),
        pl.BlockSpec((3, 2 * QK_W), lambda i, c: (0, 0)),
        chunk(M_W),
        chunk(GATE_COLS),
        pl.BlockSpec((GATE_COLS, L), lambda i, c: (0, i * nc + pos(c))),
        pl.BlockSpec((1, GATE_COLS), lambda i, c: (0, 0)),
        pl.BlockSpec((GATE_COLS, LANES), lambda i, c: (0, 0)),
    ]
    args = [zqk, zqk, zqk, conv_w, zv, gcol, grow, bcol, brow]
    scratch = [pltpu.VMEM((M_HEADS, M_DK, M_DV), F32), pltpu.VMEM((M_HEADS, M_DK, 1), F32),
               pltpu.VMEM((M_HEADS, LANES), F32)]
    if reverse:
        in_specs += [chunk(M_W), chunk(M_W), pl.BlockSpec((1, M_W), lambda i, c: (0, 0))]
        args += [hfwd, zo, hng]
        scratch += [pltpu.VMEM((L, M_W), F32)]
        out_dtype = BF16
    else:
        out_dtype = F32
    return pl.pallas_call(
        functools.partial(_mlstm_kernel, reverse, nc),
        grid=(b, nc),
        in_specs=in_specs,
        out_specs=chunk(M_W),
        out_shape=jax.ShapeDtypeStruct((b, s, M_W), out_dtype),
        scratch_shapes=scratch,
        compiler_params=_cparams(("parallel", "arbitrary"), 48),
        name="mlstm_bwd" if reverse else "mlstm_fwd",
    )(*args)


def _outproj_kernel(four_ref, mo_ref, x_ref, wof_ref, wom_ref, g_ref, b_ref, wr_ref, br_ref,
                    h_ref, route_ref, gate_ref, cnt_ref, carry_ref):
    tm = x_ref.shape[0]

    @pl.when(pl.program_id(0) == 0)
    def _():
        carry_ref[...] = jnp.zeros_like(carry_ref)

    mix = _dot(four_ref[...].astype(BF16), wof_ref[...]) + _dot(mo_ref[...], wom_ref[...])
    h = _layer_norm_rows(ALPHA * x_ref[...] + mix, g_ref[...], b_ref[...])
    h_ref[...] = h
    logits = _dot(h.astype(BF16), wr_ref[...]) + br_ref[...]
    lt = logits.T
    ri = lax.broadcasted_iota(I32, (SUBLANES, tm), 0)
    lc = jnp.where(ri < N_GROUPS, lt[0:SUBLANES], -jnp.inf)
    cmax = jnp.max(lc, axis=0, keepdims=True)
    grp = jnp.min(jnp.where(lc == cmax, ri, SUBLANES), axis=0, keepdims=True)
    p_grp = 1.0 / jnp.sum(jnp.exp(lc - cmax), axis=0, keepdims=True)
    sel = jnp.zeros((EXP_PER_GROUP, tm), F32)
    for g in range(N_GROUPS):
        lo = FINE_OFF + g * EXP_PER_GROUP
        sel = jnp.where(grp == g, lt[lo:lo + EXP_PER_GROUP], sel)
    v1 = jnp.max(sel, axis=0, keepdims=True)
    j1 = jnp.min(jnp.where(sel == v1, ri, SUBLANES), axis=0, keepdims=True)
    rest = jnp.where(ri == j1, -jnp.inf, sel)
    v2 = jnp.max(rest, axis=0, keepdims=True)
    j2 = jnp.min(jnp.where(rest == v2, ri, SUBLANES), axis=0, keepdims=True)
    e21 = jnp.exp(v2 - v1)
    g1 = p_grp / (1.0 + e21)
    g2 = p_grp * e21 / (1.0 + e21)
    eid0 = grp * EXP_PER_GROUP + j1
    eid1 = grp * EXP_PER_GROUP + j2

    ei = lax.broadcasted_iota(I32, (N_EXPERTS, tm), 0)
    oh0 = ei == eid0
    oh1 = ei == eid1
    cnt = jnp.where(oh0 | oh1, 1.0, 0.0).astype(BF16)
    rr = lax.broadcasted_iota(I32, (tm, tm), 0)
    cc = lax.broadcasted_iota(I32, (tm, tm), 1)
    before = jnp.where(rr < cc, 1.0, 0.0).astype(BF16)
    carry = carry_ref[...]
    tot = _dot(cnt, before) + carry[:, 0:1]
    rank0 = jnp.sum(jnp.where(oh0, tot, 0.0), axis=0, keepdims=True)
    rank1 = jnp.sum(jnp.where(oh1, tot, 0.0), axis=0, keepdims=True)
    new_carry = carry + jnp.sum(cnt.astype(F32), axis=1, keepdims=True)
    carry_ref[...] = new_carry
    cnt_ref[...] = new_carry.astype(I32)
    zi = jnp.zeros((SUBLANES - 4, tm), I32)
    route_ref[...] = jnp.concatenate([eid0, eid1, rank0.astype(I32), rank1.astype(I32), zi], axis=0)
    gpad = jnp.concatenate([g1, g2, jnp.zeros((LANES - 2, tm), F32)], axis=0)
    gate_ref[...] = gpad.T


def _outproj(four, mo, x2d, wof, wom, ln_g, ln_b, wr, br):
    t = x2d.shape[0]
    tm = TM_OUT
    row = lambda w: pl.BlockSpec((tm, w), lambda i: (i, 0))
    return pl.pallas_call(
        _outproj_kernel,
        grid=(t // tm,),
        in_specs=[row(F_W), row(M_W), row(D_MODEL), _resident(wof.shape), _resident(wom.shape),
                  _resident((1, D_MODEL)), _resident((1, D_MODEL)), _resident(wr.shape),
                  _resident((1, ROUTE_COLS))],
        out_specs=[row(D_MODEL), pl.BlockSpec((SUBLANES, tm), lambda i: (0, i)), row(LANES),
                   pl.BlockSpec((N_EXPERTS, LANES), lambda i: (0, 0))],
        out_shape=[jax.ShapeDtypeStruct((t, D_MODEL), F32), jax.ShapeDtypeStruct((SUBLANES, t), I32),
                   jax.ShapeDtypeStruct((t, LANES), F32), jax.ShapeDtypeStruct((N_EXPERTS, LANES), I32)],
        scratch_shapes=[pltpu.VMEM((N_EXPERTS, LANES), F32)],
        compiler_params=_cparams(("arbitrary",), 56),
        name="outproj_router",
    )(four, mo, x2d, wof, wom, ln_g, ln_b, wr, br)


def _dest_kernel(route_ref, pstart_ref, dest_ref):
    tm = route_ref.shape[1]
    r = route_ref[...]
    ei = lax.broadcasted_iota(I32, (N_EXPERTS, tm), 0)
    ps = pstart_ref[:, 0:1]
    d0 = jnp.sum(jnp.where(ei == r[0:1], ps, 0), axis=0, keepdims=True) + r[2:3]
    d1 = jnp.sum(jnp.where(ei == r[1:2], ps, 0), axis=0, keepdims=True) + r[3:4]
    dest_ref[...] = jnp.concatenate([d0, d1, jnp.zeros((SUBLANES - 2, tm), I32)], axis=0)


def _dest(route, pstart_b):
    t = route.shape[1]
    tm = 2048
    return pl.pallas_call(
        _dest_kernel,
        grid=(t // tm,),
        in_specs=[pl.BlockSpec((SUBLANES, tm), lambda i: (0, i)),
                  pl.BlockSpec((N_EXPERTS, LANES), lambda i: (0, 0))],
        out_specs=pl.BlockSpec((SUBLANES, tm), lambda i: (0, i)),
        out_shape=jax.ShapeDtypeStruct((SUBLANES, t), I32),
        compiler_params=_cparams(("parallel",), 32),
        name="moe_dest",
    )(route, pstart_b)


def _row_copy(src, dst, sem):
    return pltpu.make_async_copy(src, dst, sem)


def _pack_rows(x):
    lo = lax.bitcast_convert_type(x[:, :PACK_W].astype(BF16).astype(F32), U32)
    hi = lax.bitcast_convert_type(x[:, PACK_W:].astype(BF16).astype(F32), U32)
    return hi | (lo >> 16)


def _unpack_rows(w):
    lo = lax.bitcast_convert_type(w << 16, F32)
    hi = lax.bitcast_convert_type(w & jnp.uint32(0xFFFF0000), F32)
    return jnp.concatenate([lo, hi], axis=1)


def _rows_to_slabs(words, slab_ref):
    rows = words.shape[0]
    for j in range(ROW_SLABS):
        slab_ref[pl.ds(j, rows, stride=ROW_SLABS), :] = words[:, j * LANES:(j + 1) * LANES]


def _slabs_to_rows(slab_ref):
    rows = slab_ref.shape[0] // ROW_SLABS
    return jnp.concatenate([slab_ref[pl.ds(j, rows, stride=ROW_SLABS), :] for j in range(ROW_SLABS)], axis=1)


def _slab(ref, r):
    if isinstance(r, int):
        return ref.at[pl.ds(r * ROW_SLABS, ROW_SLABS), :]
    return ref.at[pl.ds(pl.multiple_of(r * ROW_SLABS, ROW_SLABS), ROW_SLABS), :]


def _dispatch_kernel(d0_ref, d1_ref, cnt_ref, ps_ref, nu_ref, h_ref, xs_hbm, rows_ref, zero_ref, sem):
    td = h_ref.shape[0]
    zr = zero_ref.shape[0] // ROW_SLABS
    n_blocks = xs_hbm.shape[0] // (MOE_BLK * ROW_SLABS)
    base = pl.program_id(0) * td
    _rows_to_slabs(_pack_rows(h_ref[...]), rows_ref)

    def issue(it, c):
        for u in range(DMA_UNROLL):
            r = it * DMA_UNROLL + u
            _row_copy(_slab(rows_ref, r), _slab(xs_hbm, d0_ref[base + r]), sem).start(priority=0)
            _row_copy(_slab(rows_ref, r), _slab(xs_hbm, d1_ref[base + r]), sem).start(priority=1)
        return c

    lax.fori_loop(0, td // DMA_UNROLL, issue, 0)

    def drain(it, c):
        for _ in range(2 * DMA_UNROLL):
            _row_copy(_slab(rows_ref, 0), _slab(xs_hbm, 0), sem).wait()
        return c

    lax.fori_loop(0, td // DMA_UNROLL, drain, 0)

    @pl.when(pl.program_id(0) == pl.num_programs(0) - 1)
    def _():
        zero_ref[...] = jnp.zeros_like(zero_ref)

        def per_expert(e, c):
            n = cnt_ref[e]
            npad = (n + MOE_BLK - 1) // MOE_BLK * MOE_BLK - n
            first = ps_ref[e] + n

            def pad_issue(r, c2):
                _row_copy(_slab(zero_ref, 0), _slab(xs_hbm, first + r), sem).start()
                return c2

            lax.fori_loop(0, npad, pad_issue, 0)

            def pad_drain(r, c2):
                _row_copy(_slab(zero_ref, 0), _slab(xs_hbm, 0), sem).wait()
                return c2

            lax.fori_loop(0, npad, pad_drain, 0)
            return c

        lax.fori_loop(0, N_EXPERTS, per_expert, 0)

        def tail_issue(j, c):
            row0 = pl.multiple_of((nu_ref[0] * MOE_BLK + j * zr) * ROW_SLABS, zr * ROW_SLABS)
            _row_copy(zero_ref, xs_hbm.at[pl.ds(row0, zr * ROW_SLABS), :], sem).start()
            return c

        n_tail = (n_blocks - nu_ref[0]) * (MOE_BLK // zr)
        lax.fori_loop(0, n_tail, tail_issue, 0)

        def tail_drain(j, c):
            _row_copy(zero_ref, xs_hbm.at[pl.ds(0, zr * ROW_SLABS), :], sem).wait()
            return c

        lax.fori_loop(0, n_tail, tail_drain, 0)


def _dispatch(d0, d1, counts, pstart, n_used, h, n_slots):
    t = h.shape[0]
    return pl.pallas_call(
        _dispatch_kernel,
        grid_spec=pltpu.PrefetchScalarGridSpec(
            num_scalar_prefetch=5, grid=(t // TD,),
            in_specs=[pl.BlockSpec((TD, D_MODEL), lambda i, *_: (i, 0))],
            out_specs=pl.BlockSpec(memory_space=pl.ANY),
            scratch_shapes=[pltpu.VMEM((TD * ROW_SLABS, LANES), U32), pltpu.VMEM((ZERO_ROWS * ROW_SLABS, LANES), U32),
                            pltpu.SemaphoreType.DMA]),
        out_shape=jax.ShapeDtypeStruct((n_slots * ROW_SLABS, LANES), U32),
        compiler_params=_cparams(("arbitrary",), 32),
        name="moe_dispatch",
    )(d0, d1, counts, pstart, n_used, h)


def _expert_kernel(be_ref, nu_ref, xs_ref, wg_ref, wu_ref, wd_ref, ys_ref):
    i = pl.program_id(0)

    @pl.when(i < nu_ref[0])
    def _():
        xb = _unpack_rows(_slabs_to_rows(xs_ref)).astype(BF16)
        a = _dot(xb, wg_ref[0, 0])
        u = _dot(xb, wu_ref[0, 0])
        hid = (a * jax.nn.sigmoid(a) * u).astype(BF16)
        _rows_to_slabs(_pack_rows(_dot(hid, wd_ref[0, 0])), ys_ref)

    @pl.when(i >= nu_ref[0])
    def _():
        ys_ref[...] = jnp.zeros_like(ys_ref)


def _experts(blk_e, n_used, xs, wg, wu, wd, layer):
    n_slots = xs.shape[0] // ROW_SLABS
    de = wg.shape[3]
    rows = pl.BlockSpec((MOE_BLK * ROW_SLABS, LANES), lambda i, be, nu: (jnp.minimum(i, nu[0] - 1), 0))
    return pl.pallas_call(
        _expert_kernel,
        grid_spec=pltpu.PrefetchScalarGridSpec(
            num_scalar_prefetch=2, grid=(n_slots // MOE_BLK,),
            in_specs=[rows,
                      pl.BlockSpec((1, 1, D_MODEL, de), lambda i, be, nu: (layer, be[i], 0, 0)),
                      pl.BlockSpec((1, 1, D_MODEL, de), lambda i, be, nu: (layer, be[i], 0, 0)),
                      pl.BlockSpec((1, 1, de, D_MODEL), lambda i, be, nu: (layer, be[i], 0, 0))],
            out_specs=pl.BlockSpec((MOE_BLK * ROW_SLABS, LANES), lambda i, be, nu: (i, 0))),
        out_shape=jax.ShapeDtypeStruct((n_slots * ROW_SLABS, LANES), U32),
        compiler_params=_cparams(("arbitrary",), 56),
        name="moe_experts",
    )(blk_e, n_used, xs, wg, wu, wd)


def _combine_kernel(d0_ref, d1_ref, h_ref, gate_ref, ys_hbm, g_ref, b_ref, o_ref, y0_ref, y1_ref, sem):
    td = h_ref.shape[0]
    step = pl.program_id(0)
    slot = step % 2

    def issue(tile, buf):
        base = tile * td

        def body(it, c):
            for u in range(DMA_UNROLL):
                r = it * DMA_UNROLL + u
                _row_copy(_slab(ys_hbm, d0_ref[base + r]), _slab(y0_ref.at[buf], r), sem.at[buf]).start()
                _row_copy(_slab(ys_hbm, d1_ref[base + r]), _slab(y1_ref.at[buf], r), sem.at[buf]).start()
            return c

        lax.fori_loop(0, td // DMA_UNROLL, body, 0)

    @pl.when(step == 0)
    def _():
        issue(0, 0)

    @pl.when(step + 1 < pl.num_programs(0))
    def _():
        issue(step + 1, 1 - slot)

    def drain(it, c):
        for _ in range(2 * DMA_UNROLL):
            _row_copy(_slab(ys_hbm, 0), _slab(y0_ref.at[slot], 0), sem.at[slot]).wait()
        return c

    lax.fori_loop(0, td // DMA_UNROLL, drain, 0)
    gate = gate_ref[...]
    y0 = _unpack_rows(_slabs_to_rows(y0_ref.at[slot]))
    y1 = _unpack_rows(_slabs_to_rows(y1_ref.at[slot]))
    ffn = y0 * gate[:, 0:1] + y1 * gate[:, 1:2]
    o_ref[...] = _layer_norm_rows(ALPHA * h_ref[...] + ffn, g_ref[...], b_ref[...])


def _combine(d0, d1, h, gate, ys, ln_g, ln_b):
    t = h.shape[0]
    return pl.pallas_call(
        _combine_kernel,
        grid_spec=pltpu.PrefetchScalarGridSpec(
            num_scalar_prefetch=2, grid=(t // TC,),
            in_specs=[pl.BlockSpec((TC, D_MODEL), lambda i, *_: (i, 0)),
                      pl.BlockSpec((TC, LANES), lambda i, *_: (i, 0)),
                      pl.BlockSpec(memory_space=pl.ANY),
                      pl.BlockSpec((1, D_MODEL), lambda i, *_: (0, 0)),
                      pl.BlockSpec((1, D_MODEL), lambda i, *_: (0, 0))],
            out_specs=pl.BlockSpec((TC, D_MODEL), lambda i, *_: (i, 0)),
            scratch_shapes=[pltpu.VMEM((2, TC * ROW_SLABS, LANES), U32), pltpu.VMEM((2, TC * ROW_SLABS, LANES), U32),
                            pltpu.SemaphoreType.DMA((2,))]),
        out_shape=jax.ShapeDtypeStruct((t, D_MODEL), F32),
        compiler_params=_cparams(("arbitrary",), 48),
        name="moe_combine",
    )(d0, d1, h, gate, ys, ln_g, ln_b)


def _channel_dft_matrix():
    c = np.arange(F_GW, dtype=np.int64)
    ang = 2.0 * np.pi * ((c[:, None] * c[None, :]) % F_GW) / F_GW
    m = np.concatenate([np.cos(ang), -np.sin(ang)], axis=1) / np.sqrt(F_GW)
    return jnp.asarray(m, dtype=BF16)


def _prep_layer(p, l):
    w_in = p["w_in"][l]
    o1 = F_W
    o3 = o1 + 2 * QK_W
    o4 = o3 + M_W
    o5 = o4 + M_W
    n_gate = 4 * M_HEADS
    wg = jnp.pad(w_in[:, o5:], ((0, 0), (0, GATE_COLS - n_gate)))
    bias = jnp.concatenate([p["b_igate"][l].reshape(-1), p["b_fgate"][l].reshape(-1),
                            jnp.zeros((GATE_COLS - n_gate,), F32)])
    wr = jnp.zeros((D_MODEL, ROUTE_COLS), F32)
    wr = wr.at[:, :N_GROUPS].set(p["w_coarse"][l]).at[:, FINE_OFF:FINE_OFF + N_EXPERTS].set(p["w_fine"][l])
    br = jnp.zeros((ROUTE_COLS,), F32)
    br = br.at[:N_GROUPS].set(p["b_coarse"][l]).at[FINE_OFF:FINE_OFF + N_EXPERTS].set(p["b_fine"][l])
    w_out = p["w_out"][l]
    return dict(
        wf=w_in[:, :o1].astype(BF16), wqk=w_in[:, o1:o3].astype(BF16), wv=w_in[:, o3:o4].astype(BF16),
        wo=w_in[:, o4:o5].astype(BF16), wg=wg.astype(BF16),
        conv=p["conv_qk"][l], bcol=bias[None, :], brow=jnp.broadcast_to(bias[:, None], (GATE_COLS, LANES)),
        hng=p["hn_g"][l][None, :],
        wof=w_out[:F_W].astype(BF16), wom=w_out[F_W:].astype(BF16),
        ln1g=p["ln1_g"][l][None, :], ln1b=p["ln1_b"][l][None, :],
        wr=wr.astype(BF16), br=br[None, :],
        weg=p["weg"], weu=p["weu"], wed=p["wed"], layer=l,
        ln2g=p["ln2_g"][l][None, :], ln2b=p["ln2_b"][l][None, :],
    )


def _moe(h, route, gate, counts_b, w):
    t = h.shape[0]
    n_blocks = (t * 2) // MOE_BLK + N_EXPERTS
    counts = counts_b[:, 0]
    padded = (counts + MOE_BLK - 1) // MOE_BLK * MOE_BLK
    pend = jnp.cumsum(padded)
    pstart = (pend - padded).astype(I32)
    n_used = (pend[-1] // MOE_BLK).astype(I32)
    first_row = jnp.minimum(jnp.arange(n_blocks, dtype=I32), n_used - 1) * MOE_BLK
    blk_e = jnp.sum((pend[None, :] <= first_row[:, None]).astype(I32), axis=1)
    blk_e = jnp.minimum(blk_e, N_EXPERTS - 1)
    dest = _dest(route, jnp.broadcast_to(pstart[:, None], (N_EXPERTS, LANES)))
    d0, d1 = dest[0], dest[1]
    xs = _dispatch(d0, d1, counts.astype(I32), pstart, n_used[None], h, n_blocks * MOE_BLK)
    ys = _experts(blk_e, n_used[None], xs, w["weg"], w["weu"], w["wed"], w["layer"])
    return _combine(d0, d1, h, gate, ys, w["ln2g"], w["ln2b"])


def _encode(x, ln_in_g, ln_in_b, wc, layers):
    b, s, d = x.shape
    t = b * s
    x2d = x.reshape(t, d)
    for l, w in enumerate(layers):
        outs = _inproj(x2d, ln_in_g, ln_in_b, w["wf"], wc, w["wqk"], w["wv"], w["wo"], w["wg"], l == 0)
        if l == 0:
            x2d, outs = outs[0], outs[1:]
        xr, xi, zqk, zv, zo, gcol, grow = outs
        four = _seq_dft(xr.reshape(b, s, F_W), xi.reshape(b, s, F_W)).reshape(t, F_W)
        zqk3 = zqk.reshape(b, s, 2 * QK_W)
        zv3 = zv.reshape(b, s, M_W)
        gcol3 = gcol.reshape(b, s, GATE_COLS)
        hfwd = _mlstm(False, zqk3, w["conv"], zv3, gcol3, grow, w["bcol"], w["brow"])
        mo = _mlstm(True, zqk3, w["conv"], zv3, gcol3, grow, w["bcol"], w["brow"],
                    hfwd, zo.reshape(b, s, M_W), w["hng"])
        h, route, gate, counts_b = _outproj(four, mo.reshape(t, M_W), x2d, w["wof"], w["wom"],
                                            w["ln1g"], w["ln1b"], w["wr"], w["br"])
        x2d = _moe(h, route, gate, counts_b, w)
    return x2d.reshape(b, s, d)


def kernel(x_prompt, x_sample, ln_in_g, ln_in_b, w_in, conv_qk, b_igate, b_fgate, hn_g, w_out, ln1_g, ln1_b,
           w_coarse, b_coarse, w_fine, b_fine, w_e_gate, w_e_up, w_e_down, ln2_g, ln2_b):
    p = dict(w_in=w_in, conv_qk=conv_qk, b_igate=b_igate, b_fgate=b_fgate, hn_g=hn_g, w_out=w_out,
             ln1_g=ln1_g, ln1_b=ln1_b, w_coarse=w_coarse, b_coarse=b_coarse, w_fine=w_fine, b_fine=b_fine,
             weg=w_e_gate.astype(BF16), weu=w_e_up.astype(BF16), wed=w_e_down.astype(BF16),
             ln2_g=ln2_g, ln2_b=ln2_b)
    layers = [_prep_layer(p, l) for l in range(w_in.shape[0])]
    wc = _channel_dft_matrix()
    g_in, b_in = ln_in_g[None, :], ln_in_b[None, :]
    y_prompt = _encode(x_prompt, g_in, b_in, wc, layers)
    y_sample = _encode(x_sample, g_in, b_in, wc, layers)
    return (y_prompt, y_sample)
```
